```python
import jax, jax.numpy as jnp
from jax import lax
import numpy as np

D_MODEL = 2048
BATCH = 2
SEQ = 4096
DEPTH = 2

N_A_LAYERS = DEPTH // 2
N_B_LAYERS = DEPTH - N_A_LAYERS

CHUNK = 128
A_EXPAND = 2
A_WIDTH = A_EXPAND * D_MODEL
A_GROUPS = 8
A_GROUP_DIM = A_WIDTH // A_GROUPS

B_HEADS = 16
B_HEAD_DIM = D_MODEL // B_HEADS
B_WIDTH = B_HEADS * B_HEAD_DIM
Q_BLOCK = 128

DN_ALPHA = (2.0 * DEPTH) ** 0.25
DN_BETA = (8.0 * DEPTH) ** -0.25
LN_EPS = 1e-5

kernel_name = "yoco_gmlp_stickbreaking_deepnorm"


def layer_norm(x, g, b):
    xf = x.astype(jnp.float32)
    mu = jnp.mean(xf, axis=-1, keepdims=True)
    xc = xf - mu
    var = jnp.mean(xc * xc, axis=-1, keepdims=True)
    y = xc * lax.rsqrt(var + LN_EPS) * g.astype(jnp.float32) + b.astype(jnp.float32)
    return y.astype(x.dtype)


def gmlp_mixer(x, w_in, b_in, vln_g, vln_b, w_s, b_s, w_out):
    bsz, seq, _ = x.shape
    h = x @ w_in + b_in
    u, v, g = jnp.split(h, 3, axis=-1)
    u = jax.nn.gelu(u)
    v = layer_norm(jax.nn.gelu(v), vln_g, vln_b)
    n_chunks = seq // CHUNK
    v = v.reshape(bsz, n_chunks, CHUNK, A_GROUPS, A_GROUP_DIM)
    causal = jnp.tril(jnp.ones((CHUNK, CHUNK), dtype=bool))
    w = jnp.where(causal[None], w_s, jnp.zeros_like(w_s))
    mixed = jnp.einsum('gts,bcsge->bctge', w, v) + b_s.T[None, None, :, :, None]
    s = u * mixed.reshape(bsz, seq, A_WIDTH)
    return (s * jax.nn.silu(g)) @ w_out


def stick_breaking_attention(q, k, v):
    seq = q.shape[2]
    scale = B_HEAD_DIM ** -0.5
    outs = []
    for blk in range(seq // Q_BLOCK):
        q0 = blk * Q_BLOCK
        kend = q0 + Q_BLOCK
        qb = q[:, :, q0:kend].astype(jnp.float32)
        kb = k[:, :, :kend].astype(jnp.float32)
        vb = v[:, :, :kend].astype(jnp.float32)
        z = jnp.einsum('bhtd,bhsd->bhts', qb, kb) * scale
        t_idx = q0 + jnp.arange(Q_BLOCK)[:, None]
        s_idx = jnp.arange(kend)[None, :]
        past = s_idx < t_idx
        log_keep = jnp.where(past, -jax.nn.softplus(z), 0.0)
        after = lax.cumsum(log_keep, axis=log_keep.ndim - 1, reverse=True) - log_keep
        log_w = jax.nn.log_sigmoid(z) + after
        wts = jnp.where(past, jnp.exp(log_w), 0.0)
        outs.append(jnp.einsum('bhts,bhsd->bhtd', wts, vb))
    return jnp.concatenate(outs, axis=2)


def stick_breaking_mixer(x, k, v, w_in, w_out):
    bsz, seq, _ = x.shape
    h = x @ w_in
    q, g = jnp.split(h, 2, axis=-1)
    q = q.reshape(bsz, seq, B_HEADS, B_HEAD_DIM).transpose(0, 2, 1, 3)
    o = stick_breaking_attention(q, k, v).astype(x.dtype)
    o = o.transpose(0, 2, 1, 3).reshape(bsz, seq, B_WIDTH)
    return (o * jax.nn.silu(g)) @ w_out


def setup_inputs(seed: int = 0) -> dict:
    key = jax.random.key(seed)
    ks = jax.random.split(key, 16)
    f32 = jnp.float32
    x = jax.random.normal(ks[0], (BATCH, SEQ, D_MODEL), f32)
    a_w_in = jax.random.normal(ks[1], (N_A_LAYERS, D_MODEL, 3 * A_WIDTH), f32) * D_MODEL ** -0.5
    a_b_in = 0.02 * jax.random.normal(ks[2], (N_A_LAYERS, 3 * A_WIDTH), f32)
    a_vln_g = 1.0 + 0.02 * jax.random.normal(ks[3], (N_A_LAYERS, A_WIDTH), f32)
    a_vln_b = 0.02 * jax.random.normal(ks[4], (N_A_LAYERS, A_WIDTH), f32)
    a_w_s = jax.random.normal(ks[5], (N_A_LAYERS, A_GROUPS, CHUNK, CHUNK), f32) * CHUNK ** -0.5
    a_b_s = 1.0 + 0.02 * jax.random.normal(ks[6], (N_A_LAYERS, A_GROUPS, CHUNK), f32)
    a_w_out = jax.random.normal(ks[7], (N_A_LAYERS, A_WIDTH, D_MODEL), f32) * (A_WIDTH ** -0.5 * DN_BETA)
    kv_w = jax.random.normal(ks[8], (D_MODEL, 2 * B_WIDTH), f32) * D_MODEL ** -0.5
    b_w_in = jax.random.normal(ks[9], (N_B_LAYERS, D_MODEL, 2 * B_WIDTH), f32) * D_MODEL ** -0.5
    b_w_out = jax.random.normal(ks[10], (N_B_LAYERS, B_WIDTH, D_MODEL), f32) * (B_WIDTH ** -0.5 * DN_BETA)
    ln_g = 1.0 + 0.02 * jax.random.normal(ks[11], (DEPTH, D_MODEL), f32)
    ln_b = 0.02 * jax.random.normal(ks[12], (DEPTH, D_MODEL), f32)
    return {"x": x, "a_w_in": a_w_in, "a_b_in": a_b_in, "a_vln_g": a_vln_g, "a_vln_b": a_vln_b,
            "a_w_s": a_w_s, "a_b_s": a_b_s, "a_w_out": a_w_out, "kv_w": kv_w,
            "b_w_in": b_w_in, "b_w_out": b_w_out, "ln_g": ln_g, "ln_b": ln_b}


def reference(x, a_w_in, a_b_in, a_vln_g, a_vln_b, a_w_s, a_b_s, a_w_out, kv_w,
              b_w_in, b_w_out, ln_g, ln_b):
    bsz, seq, _ = x.shape
    k = None
    v = None
    for layer in range(DEPTH):
        if layer < N_A_LAYERS:
            i = layer
            y = gmlp_mixer(x, a_w_in[i], a_b_in[i], a_vln_g[i], a_vln_b[i],
                           a_w_s[i], a_b_s[i], a_w_out[i])
        else:
            if layer == N_A_LAYERS:
                kv = x @ kv_w
                k, v = jnp.split(kv, 2, axis=-1)
                k = k.reshape(bsz, seq, B_HEADS, B_HEAD_DIM).transpose(0, 2, 1, 3)
                v = v.reshape(bsz, seq, B_HEADS, B_HEAD_DIM).transpose(0, 2, 1, 3)
            j = layer - N_A_LAYERS
            y = stick_breaking_mixer(x, k, v, b_w_in[j], b_w_out[j])
        x = layer_norm(DN_ALPHA * x + y, ln_g[layer], ln_b[layer])
    return x
```

```python
import functools

import jax
import jax.numpy as jnp
from jax import lax
from jax.experimental import pallas as pl
from jax.experimental.pallas import tpu as pltpu

D_MODEL = 2048
DEPTH = 2
CHUNK = 128
A_WIDTH = 2 * D_MODEL
A_GROUPS = 8
A_GROUP_DIM = A_WIDTH // A_GROUPS
B_HEADS = 16
B_HEAD_DIM = D_MODEL // B_HEADS
B_WIDTH = B_HEADS * B_HEAD_DIM
DN_ALPHA = (2.0 * DEPTH) ** 0.25
LN_EPS = 1e-5

VMEM_LIMIT_BYTES = 56 * 1024 * 1024

PROJ_TM = 1024
PROJ_TN = 1024
MIX_TM = 256
ATT_T = 256

_GELU_C = 0.7978845608028654


def _gelu_tanh(x):
    return 0.5 * x * (1.0 + jnp.tanh(_GELU_C * (x + 0.044715 * (x * x * x))))


def _silu(x):
    return 0.5 * x * (1.0 + jnp.tanh(0.5 * x))


def _layer_norm(z, g, b):
    mu = jnp.mean(z, axis=-1, keepdims=True)
    zc = z - mu
    var = jnp.mean(zc * zc, axis=-1, keepdims=True)
    return zc * lax.rsqrt(var + LN_EPS) * g + b


def _params(*semantics):
    return pltpu.CompilerParams(dimension_semantics=semantics, vmem_limit_bytes=VMEM_LIMIT_BYTES)


def _gmlp_in_kernel(x_ref, w_ref, b_ref, o_ref, *, silu_from):
    j = pl.program_id(1)
    h = jnp.dot(x_ref[...], w_ref[...], preferred_element_type=jnp.float32) + b_ref[...]

    @pl.when(j < silu_from)
    def _():
        o_ref[...] = _gelu_tanh(h).astype(o_ref.dtype)

    @pl.when(j >= silu_from)
    def _():
        o_ref[...] = _silu(h).astype(o_ref.dtype)


def _gmlp_in(x, w, b):
    n, d = x.shape
    width = w.shape[1]
    grid = (n // PROJ_TM, width // PROJ_TN)
    return pl.pallas_call(
        functools.partial(_gmlp_in_kernel, silu_from=2 * A_WIDTH // PROJ_TN),
        grid=grid,
        in_specs=[
            pl.BlockSpec((PROJ_TM, d), lambda i, j: (i, 0)),
            pl.BlockSpec((d, PROJ_TN), lambda i, j: (0, j)),
            pl.BlockSpec((1, PROJ_TN), lambda i, j: (0, j)),
        ],
        out_specs=pl.BlockSpec((PROJ_TM, PROJ_TN), lambda i, j: (i, j)),
        out_shape=jax.ShapeDtypeStruct((n, width), jnp.bfloat16),
        compiler_params=_params("parallel", "arbitrary"),
        name="gmlp_in",
    )(x, w, b)


def _gmlp_out_kernel(u_ref, v_ref, g_ref, x_ref, vg_ref, vb_ref, ws_ref, bs_ref, wo_ref,
                     lg_ref, lb_ref, o_ref, ob_ref, s_ref):
    tm = u_ref.shape[0]
    vn = _layer_norm(v_ref[...].astype(jnp.float32), vg_ref[...], vb_ref[...]).astype(jnp.bfloat16)
    row = lax.broadcasted_iota(jnp.int32, (CHUNK, CHUNK), 0)
    col = lax.broadcasted_iota(jnp.int32, (CHUNK, CHUNK), 1)
    causal = col <= row
    for g in range(A_GROUPS):
        wt = jnp.where(causal, ws_ref[g], 0.0).astype(jnp.bfloat16)
        bias = bs_ref[g]
        cols = slice(g * A_GROUP_DIM, (g + 1) * A_GROUP_DIM)
        for c in range(tm // CHUNK):
            rows = slice(c * CHUNK, (c + 1) * CHUNK)
            mixed = jnp.dot(wt, vn[rows, cols], preferred_element_type=jnp.float32) + bias
            gate = u_ref[rows, cols].astype(jnp.float32) * g_ref[rows, cols].astype(jnp.float32)
            s_ref[rows, cols] = (gate * mixed).astype(jnp.bfloat16)
    y = jnp.dot(s_ref[...], wo_ref[...], preferred_element_type=jnp.float32)
    out = _layer_norm(DN_ALPHA * x_ref[...] + y, lg_ref[...], lb_ref[...])
    o_ref[...] = out
    ob_ref[...] = out.astype(jnp.bfloat16)


def _resident(shape):
    zeros = (0,) * len(shape)
    return pl.BlockSpec(shape, lambda i: zeros, pipeline_mode=pl.Buffered(1))


def _gmlp_out(hact, x, vln_g, vln_b, w_s, b_s, w_out, ln_g, ln_b):
    n, d = x.shape
    tm = MIX_TM
    return pl.pallas_call(
        _gmlp_out_kernel,
        grid=(n // tm,),
        in_specs=[
            pl.BlockSpec((tm, A_WIDTH), lambda i: (i, 0)),
            pl.BlockSpec((tm, A_WIDTH), lambda i: (i, 1)),
            pl.BlockSpec((tm, A_WIDTH), lambda i: (i, 2)),
            pl.BlockSpec((tm, d), lambda i: (i, 0)),
            _resident((1, A_WIDTH)),
            _resident((1, A_WIDTH)),
            _resident((A_GROUPS, CHUNK, CHUNK)),
            _resident((A_GROUPS, CHUNK, 1)),
            _resident((A_WIDTH, d)),
            _resident((1, d)),
            _resident((1, d)),
        ],
        out_specs=[
            pl.BlockSpec((tm, d), lambda i: (i, 0)),
            pl.BlockSpec((tm, d), lambda i: (i, 0)),
        ],
        out_shape=[
            jax.ShapeDtypeStruct((n, d), jnp.float32),
            jax.ShapeDtypeStruct((n, d), jnp.bfloat16),
        ],
        scratch_shapes=[pltpu.VMEM((tm, A_WIDTH), jnp.bfloat16)],
        compiler_params=_params("parallel"),
        name="gmlp_out",
    )(hact, hact, hact, x, vln_g, vln_b, w_s, b_s, w_out, ln_g, ln_b)


def _attn_in_kernel(x_ref, w_ref, o_ref, *, q_from, gate_from, q_scale):
    j = pl.program_id(1)
    h = jnp.dot(x_ref[...], w_ref[...], preferred_element_type=jnp.float32)

    @pl.when(j < q_from)
    def _():
        o_ref[...] = h.astype(o_ref.dtype)

    @pl.when(jnp.logical_and(j >= q_from, j < gate_from))
    def _():
        o_ref[...] = (h * q_scale).astype(o_ref.dtype)

    @pl.when(j >= gate_from)
    def _():
        o_ref[...] = _silu(h).astype(o_ref.dtype)


def _attn_in(xb, w):
    n, d = xb.shape
    width = w.shape[1]
    kernel = functools.partial(
        _attn_in_kernel,
        q_from=2 * B_WIDTH // PROJ_TN,
        gate_from=3 * B_WIDTH // PROJ_TN,
        q_scale=B_HEAD_DIM ** -0.5,
    )
    return pl.pallas_call(
        kernel,
        grid=(n // PROJ_TM, width // PROJ_TN),
        in_specs=[
            pl.BlockSpec((PROJ_TM, d), lambda i, j: (i, 0)),
            pl.BlockSpec((d, PROJ_TN), lambda i, j: (0, j)),
        ],
        out_specs=pl.BlockSpec((PROJ_TM, PROJ_TN), lambda i, j: (i, j)),
        out_shape=jax.ShapeDtypeStruct((n, width), jnp.bfloat16),
        compiler_params=_params("parallel", "arbitrary"),
        name="attn_in",
    )(xb, w)


def _attn_kernel(q_ref, k_ref, v_ref, m_ref, o_ref):
    t = ATT_T
    qi = pl.program_id(2)
    q = q_ref[...]
    neg_tri = m_ref[...]

    def tile(kb, c, acc, masked):
        start = pl.multiple_of(kb * t, t)
        k_blk = k_ref[pl.ds(start, t), :]
        v_blk = v_ref[pl.ds(start, t), :]
        z = lax.dot_general(q, k_blk, (((1,), (1,)), ((), ())), preferred_element_type=jnp.float32)
        sp = jnp.maximum(z, 0.0) + jnp.log(1.0 + jnp.exp(-jnp.abs(z)))
        log_sig = z - sp
        if masked:
            past = (lax.broadcasted_iota(jnp.int32, (t, t), 1)
                    < lax.broadcasted_iota(jnp.int32, (t, t), 0))
            sp = jnp.where(past, sp, 0.0)
        hi = sp.astype(jnp.bfloat16)
        lo = (sp - hi.astype(jnp.float32)).astype(jnp.bfloat16)
        after = (jnp.dot(hi, neg_tri, preferred_element_type=jnp.float32)
                 + jnp.dot(lo, neg_tri, preferred_element_type=jnp.float32))
        w = jnp.exp(log_sig + after + c)
        if masked:
            w = jnp.where(past, w, 0.0)
        acc = acc + jnp.dot(w.astype(jnp.bfloat16), v_blk, preferred_element_type=jnp.float32)
        c = c - jnp.sum(sp, axis=1, keepdims=True)
        return c, acc

    c0 = jnp.zeros((t, 1), jnp.float32)
    acc0 = jnp.zeros((t, B_HEAD_DIM), jnp.float32)
    c, acc = tile(qi, c0, acc0, True)

    def body(step, carry):
        return tile(qi - 1 - step, carry[0], carry[1], False)

    c, acc = lax.fori_loop(0, qi, body, (c, acc))
    o_ref[...] = acc.astype(o_ref.dtype)


def _attention(proj, neg_tri, bsz, seq):
    t = ATT_T
    nq = seq // t
    dh = B_HEAD_DIM
    return pl.pallas_call(
        _attn_kernel,
        grid=(bsz, B_HEADS, nq),
        in_specs=[
            pl.BlockSpec((t, dh), lambda b, h, i: (b * nq + i, 2 * B_HEADS + h)),
            pl.BlockSpec((seq, dh), lambda b, h, i: (b, h)),
            pl.BlockSpec((seq, dh), lambda b, h, i: (b, B_HEADS + h)),
            pl.BlockSpec((t, t), lambda b, h, i: (0, 0)),
        ],
        out_specs=pl.BlockSpec((t, dh), lambda b, h, i: (b * nq + i, h)),
        out_shape=jax.ShapeDtypeStruct((bsz * seq, B_WIDTH), jnp.bfloat16),
        compiler_params=_params("parallel", "parallel", "arbitrary"),
        name="stick_breaking_attention",
    )(proj, proj, proj, neg_tri)


def _attn_out_kernel(o_ref, g_ref, x_ref, wo_ref, lg_ref, lb_ref, out_ref):
    s = (o_ref[...].astype(jnp.float32) * g_ref[...].astype(jnp.float32)).astype(jnp.bfloat16)
    y = jnp.dot(s, wo_ref[...], preferred_element_type=jnp.float32)
    out_ref[...] = _layer_norm(DN_ALPHA * x_ref[...] + y, lg_ref[...], lb_ref[...])


def _attn_out(o, proj, x, w_out, ln_g, ln_b):
    n, d = x.shape
    tm = MIX_TM
    return pl.pallas_call(
        _attn_out_kernel,
        grid=(n // tm,),
        in_specs=[
            pl.BlockSpec((tm, B_WIDTH), lambda i: (i, 0)),
            pl.BlockSpec((tm, B_WIDTH), lambda i: (i, 3)),
            pl.BlockSpec((tm, d), lambda i: (i, 0)),
            _resident((B_WIDTH, d)),
            _resident((1, d)),
            _resident((1, d)),
        ],
        out_specs=pl.BlockSpec((tm, d), lambda i: (i, 0)),
        out_shape=jax.ShapeDtypeStruct((n, d), jnp.float32),
        compiler_params=_params("parallel"),
        name="attn_out",
    )(o, proj, x, w_out, ln_g, ln_b)


def kernel(x, a_w_in, a_b_in, a_vln_g, a_vln_b, a_w_s, a_b_s, a_w_out, kv_w, b_w_in, b_w_out, ln_g, ln_b):
    bsz, seq, d = x.shape
    n = bsz * seq
    bf16 = jnp.bfloat16
    n_a = a_w_in.shape[0]
    n_b = b_w_in.shape[0]

    xf = x.reshape(n, d)
    xb = xf.astype(bf16)
    for i in range(n_a):
        hact = _gmlp_in(xb, a_w_in[i].astype(bf16), a_b_in[i][None, :])
        xf, xb = _gmlp_out(hact, xf, a_vln_g[i][None, :], a_vln_b[i][None, :], a_w_s[i],
                           a_b_s[i][:, :, None], a_w_out[i].astype(bf16),
                           ln_g[i][None, :], ln_b[i][None, :])

    idx = jnp.arange(ATT_T)
    neg_tri = jnp.where(idx[:, None] > idx[None, :], -1.0, 0.0).astype(bf16)
    assert n_b == 1, "the fused k|v|q|gate projection supports exactly one attention layer"
    w_b = jnp.concatenate([kv_w.astype(bf16), b_w_in[0].astype(bf16)], axis=1)
    proj = _attn_in(xb, w_b)
    o = _attention(proj, neg_tri, bsz, seq)
    xf = _attn_out(o, proj, xf, b_w_out[0].astype(bf16), ln_g[n_a][None, :], ln_b[n_a][None, :])
    return xf.reshape(bsz, seq, d)
```

```python
import functools

import jax
import jax.numpy as jnp
from jax import lax
from jax.experimental import pallas as pl
from jax.experimental.pallas import tpu as pltpu

D_MODEL = 2048
DEPTH = 2
CHUNK = 128
A_WIDTH = 2 * D_MODEL
A_GROUPS = 8
A_GROUP_DIM = A_WIDTH // A_GROUPS
B_HEADS = 16
B_HEAD_DIM = D_MODEL // B_HEADS
B_WIDTH = B_HEADS * B_HEAD_DIM
DN_ALPHA = (2.0 * DEPTH) ** 0.25
LN_EPS = 1e-5

VMEM_LIMIT_BYTES = 56 * 1024 * 1024

PROJ_TM = 1024
PROJ_TN = 1024
MIX_TM = 256
ATT_T = 256
ATT_HEADS = 4

_GELU_C = 0.7978845608028654
_LOG2_E = 1.4426950408889634


def _gelu_tanh(x):
    return 0.5 * x * (1.0 + jnp.tanh(_GELU_C * (x + 0.044715 * (x * x * x))))


def _silu(x):
    return 0.5 * x * (1.0 + jnp.tanh(0.5 * x))


def _layer_norm(z, g, b):
    mu = jnp.mean(z, axis=-1, keepdims=True)
    zc = z - mu
    var = jnp.mean(zc * zc, axis=-1, keepdims=True)
    return zc * lax.rsqrt(var + LN_EPS) * g + b


def _params(*semantics):
    return pltpu.CompilerParams(dimension_semantics=semantics, vmem_limit_bytes=VMEM_LIMIT_BYTES)


def _gmlp_in_kernel(x_ref, w_ref, b_ref, o_ref, *, silu_from):
    j = pl.program_id(1)
    h = jnp.dot(x_ref[...], w_ref[...], preferred_element_type=jnp.float32) + b_ref[...]

    @pl.when(j < silu_from)
    def _():
        o_ref[...] = _gelu_tanh(h).astype(o_ref.dtype)

    @pl.when(j >= silu_from)
    def _():
        o_ref[...] = _silu(h).astype(o_ref.dtype)


def _gmlp_in(x, w, b):
    n, d = x.shape
    width = w.shape[1]
    grid = (n // PROJ_TM, width // PROJ_TN)
    return pl.pallas_call(
        functools.partial(_gmlp_in_kernel, silu_from=2 * A_WIDTH // PROJ_TN),
        grid=grid,
        in_specs=[
            pl.BlockSpec((PROJ_TM, d), lambda i, j: (i, 0)),
            pl.BlockSpec((d, PROJ_TN), lambda i, j: (0, j)),
            pl.BlockSpec((1, PROJ_TN), lambda i, j: (0, j)),
        ],
        out_specs=pl.BlockSpec((PROJ_TM, PROJ_TN), lambda i, j: (i, j)),
        out_shape=jax.ShapeDtypeStruct((n, width), jnp.bfloat16),
        compiler_params=_params("parallel", "arbitrary"),
        name="gmlp_in",
    )(x, w, b)


def _gmlp_out_kernel(u_ref, v_ref, g_ref, x_ref, vg_ref, vb_ref, ws_ref, bs_ref, wo_ref,
                     lg_ref, lb_ref, o_ref, ob_ref, s_ref):
    tm = u_ref.shape[0]
    vn = _layer_norm(v_ref[...].astype(jnp.float32), vg_ref[...], vb_ref[...]).astype(jnp.bfloat16)
    row = lax.broadcasted_iota(jnp.int32, (CHUNK, CHUNK), 0)
    col = lax.broadcasted_iota(jnp.int32, (CHUNK, CHUNK), 1)
    causal = col <= row
    for g in range(A_GROUPS):
        wt = jnp.where(causal, ws_ref[g], 0.0).astype(jnp.bfloat16)
        bias = bs_ref[g]
        cols = slice(g * A_GROUP_DIM, (g + 1) * A_GROUP_DIM)
        for c in range(tm // CHUNK):
            rows = slice(c * CHUNK, (c + 1) * CHUNK)
            mixed = jnp.dot(wt, vn[rows, cols], preferred_element_type=jnp.float32) + bias
            gate = u_ref[rows, cols].astype(jnp.float32) * g_ref[rows, cols].astype(jnp.float32)
            s_ref[rows, cols] = (gate * mixed).astype(jnp.bfloat16)
    y = jnp.dot(s_ref[...], wo_ref[...], preferred_element_type=jnp.float32)
    out = _layer_norm(DN_ALPHA * x_ref[...] + y, lg_ref[...], lb_ref[...])
    o_ref[...] = out
    ob_ref[...] = out.astype(jnp.bfloat16)


def _resident(shape):
    zeros = (0,) * len(shape)
    return pl.BlockSpec(shape, lambda i: zeros, pipeline_mode=pl.Buffered(1))


def _gmlp_out(hact, x, vln_g, vln_b, w_s, b_s, w_out, ln_g, ln_b):
    n, d = x.shape
    tm = MIX_TM
    return pl.pallas_call(
        _gmlp_out_kernel,
        grid=(n // tm,),
        in_specs=[
            pl.BlockSpec((tm, A_WIDTH), lambda i: (i, 0)),
            pl.BlockSpec((tm, A_WIDTH), lambda i: (i, 1)),
            pl.BlockSpec((tm, A_WIDTH), lambda i: (i, 2)),
            pl.BlockSpec((tm, d), lambda i: (i, 0)),
            _resident((1, A_WIDTH)),
            _resident((1, A_WIDTH)),
            _resident((A_GROUPS, CHUNK, CHUNK)),
            _resident((A_GROUPS, CHUNK, 1)),
            _resident((A_WIDTH, d)),
            _resident((1, d)),
            _resident((1, d)),
        ],
        out_specs=[
            pl.BlockSpec((tm, d), lambda i: (i, 0)),
            pl.BlockSpec((tm, d), lambda i: (i, 0)),
        ],
        out_shape=[
            jax.ShapeDtypeStruct((n, d), jnp.float32),
            jax.ShapeDtypeStruct((n, d), jnp.bfloat16),
        ],
        scratch_shapes=[pltpu.VMEM((tm, A_WIDTH), jnp.bfloat16)],
        compiler_params=_params("parallel"),
        name="gmlp_out",
    )(hact, hact, hact, x, vln_g, vln_b, w_s, b_s, w_out, ln_g, ln_b)


def _attn_in_kernel(x_ref, w_ref, o_ref, *, q_from, gate_from, q_scale):
    j = pl.program_id(1)
    h = jnp.dot(x_ref[...], w_ref[...], preferred_element_type=jnp.float32)

    @pl.when(j < q_from)
    def _():
        o_ref[...] = h.astype(o_ref.dtype)

    @pl.when(jnp.logical_and(j >= q_from, j < gate_from))
    def _():
        o_ref[...] = (h * q_scale).astype(o_ref.dtype)

    @pl.when(j >= gate_from)
    def _():
        o_ref[...] = _silu(h).astype(o_ref.dtype)


def _attn_in(xb, w):
    n, d = xb.shape
    width = w.shape[1]
    kernel = functools.partial(
        _attn_in_kernel,
        q_from=2 * B_WIDTH // PROJ_TN,
        gate_from=3 * B_WIDTH // PROJ_TN,
        q_scale=B_HEAD_DIM ** -0.5 * _LOG2_E,
    )
    return pl.pallas_call(
        kernel,
        grid=(n // PROJ_TM, width // PROJ_TN),
        in_specs=[
            pl.BlockSpec((PROJ_TM, d), lambda i, j: (i, 0)),
            pl.BlockSpec((d, PROJ_TN), lambda i, j: (0, j)),
        ],
        out_specs=pl.BlockSpec((PROJ_TM, PROJ_TN), lambda i, j: (i, j)),
        out_shape=jax.ShapeDtypeStruct((n, width), jnp.bfloat16),
        compiler_params=_params("parallel", "arbitrary"),
        name="attn_in",
    )(xb, w)


def _neg_abs(z):
    bits = lax.bitcast_convert_type(z, jnp.int32) | jnp.int32(-2 ** 31)
    return lax.bitcast_convert_type(bits, jnp.float32)


def _attn_kernel(q_ref, k_ref, v_ref, m_ref, o_ref):
    t = ATT_T
    dh = B_HEAD_DIM
    qi = pl.program_id(2)
    neg_tri = m_ref[...]

    def sweep(kb, carry, masked):
        heads = range(ATT_HEADS)
        start = pl.multiple_of(kb * t, t)
        if masked:
            past = (lax.broadcasted_iota(jnp.int32, (t, t), 1)
                    < lax.broadcasted_iota(jnp.int32, (t, t), 0))
        z = [lax.dot_general(q_ref[:, h * dh:(h + 1) * dh], k_ref[pl.ds(start, t), h * dh:(h + 1) * dh],
                             (((1,), (1,)), ((), ())), preferred_element_type=jnp.float32)
             for h in heads]
        suffix = []
        for h in heads:
            sp = jnp.maximum(z[h], 0.0) + jnp.log2(1.0 + jnp.exp2(_neg_abs(z[h])))
            if masked:
                sp = jnp.where(past, sp, 0.0)
            hi = sp.astype(jnp.bfloat16)
            lo = (sp - hi.astype(jnp.float32)).astype(jnp.bfloat16)
            suffix.append(jnp.dot(hi, neg_tri, preferred_element_type=jnp.float32)
                          + jnp.dot(lo, neg_tri, preferred_element_type=jnp.float32))
        out = []
        for h in heads:
            c, acc = carry[h]
            w = jnp.exp2(z[h] + suffix[h] + c)
            if masked:
                w = jnp.where(past, w, 0.0)
            acc = acc + jnp.dot(w.astype(jnp.bfloat16), v_ref[pl.ds(start, t), h * dh:(h + 1) * dh],
                                preferred_element_type=jnp.float32)
            out.append((c + suffix[h][:, 0:1], acc))
        return tuple(out)

    zero = (jnp.zeros((t, 1), jnp.float32), jnp.zeros((t, dh), jnp.float32))
    carry = sweep(qi, (zero,) * ATT_HEADS, True)
    carry = lax.fori_loop(0, qi, lambda step, carry: sweep(qi - 1 - step, carry, False), carry)
    for h in range(ATT_HEADS):
        o_ref[:, h * dh:(h + 1) * dh] = carry[h][1].astype(o_ref.dtype)


def _attention(proj, neg_tri, bsz, seq):
    t = ATT_T
    nq = seq // t
    hw = ATT_HEADS * B_HEAD_DIM
    nh = B_WIDTH // hw
    return pl.pallas_call(
        _attn_kernel,
        grid=(bsz, nh, nq),
        in_specs=[
            pl.BlockSpec((t, hw), lambda b, h, i: (b * nq + i, 2 * nh + h)),
            pl.BlockSpec((seq, hw), lambda b, h, i: (b, h)),
            pl.BlockSpec((seq, hw), lambda b, h, i: (b, nh + h)),
            pl.BlockSpec((t, t), lambda b, h, i: (0, 0)),
        ],
        out_specs=pl.BlockSpec((t, hw), lambda b, h, i: (b * nq + i, h)),
        out_shape=jax.ShapeDtypeStruct((bsz * seq, B_WIDTH), jnp.bfloat16),
        compiler_params=_params("parallel", "parallel", "arbitrary"),
        name="stick_breaking_attention",
    )(proj, proj, proj, neg_tri)


def _attn_out_kernel(o_ref, g_ref, x_ref, wo_ref, lg_ref, lb_ref, out_ref):
    s = (o_ref[...].astype(jnp.float32) * g_ref[...].astype(jnp.float32)).astype(jnp.bfloat16)
    y = jnp.dot(s, wo_ref[...], preferred_element_type=jnp.float32)
    out_ref[...] = _layer_norm(DN_ALPHA * x_ref[...] + y, lg_ref[...], lb_ref[...])


def _attn_out(o, proj, x, w_out, ln_g, ln_b):
    n, d = x.shape
    tm = MIX_TM
    return pl.pallas_call(
        _attn_out_kernel,
        grid=(n // tm,),
        in_specs=[
            pl.BlockSpec((tm, B_WIDTH), lambda i: (i, 0)),
            pl.BlockSpec((tm, B_WIDTH), lambda i: (i, 3)),
            pl.BlockSpec((tm, d), lambda i: (i, 0)),
            _resident((B_WIDTH, d)),
            _resident((1, d)),
            _resident((1, d)),
        ],
        out_specs=pl.BlockSpec((tm, d), lambda i: (i, 0)),
        out_shape=jax.ShapeDtypeStruct((n, d), jnp.float32),
        compiler_params=_params("parallel"),
        name="attn_out",
    )(o, proj, x, w_out, ln_g, ln_b)


def kernel(x, a_w_in, a_b_in, a_vln_g, a_vln_b, a_w_s, a_b_s, a_w_out, kv_w, b_w_in, b_w_out, ln_g, ln_b):
    bsz, seq, d = x.shape
    n = bsz * seq
    bf16 = jnp.bfloat16
    n_a = a_w_in.shape[0]
    n_b = b_w_in.shape[0]

    xf = x.reshape(n, d)
    xb = xf.astype(bf16)
    for i in range(n_a):
        hact = _gmlp_in(xb, a_w_in[i].astype(bf16), a_b_in[i][None, :])
        xf, xb = _gmlp_out(hact, xf, a_vln_g[i][None, :], a_vln_b[i][None, :], a_w_s[i],
                           a_b_s[i][:, :, None], a_w_out[i].astype(bf16),
                           ln_g[i][None, :], ln_b[i][None, :])

    idx = jnp.arange(ATT_T)
    neg_tri = jnp.where(idx[:, None] >= idx[None, :], -1.0, 0.0).astype(bf16)
    assert n_b == 1, "the fused k|v|q|gate projection supports exactly one attention layer"
    w_b = jnp.concatenate([kv_w.astype(bf16), b_w_in[0].astype(bf16)], axis=1)
    proj = _attn_in(xb, w_b)
    o = _attention(proj, neg_tri, bsz, seq)
    xf = _attn_out(o, proj, xf, b_w_out[0].astype(bf16), ln_g[n_a][None, :], ln_b[n_a][None, :])
    return xf.reshape(bsz, seq, d)
```

```python
import functools

import jax
import jax.numpy as jnp
from jax import lax
from jax.experimental import pallas as pl
from jax.experimental.pallas import tpu as pltpu

D_MODEL = 2048
DEPTH = 2
CHUNK = 128
A_WIDTH = 2 * D_MODEL
A_GROUPS = 8
A_GROUP_DIM = A_WIDTH // A_GROUPS
B_HEADS = 16
B_HEAD_DIM = D_MODEL // B_HEADS
B_WIDTH = B_HEADS * B_HEAD_DIM
DN_ALPHA = (2.0 * DEPTH) ** 0.25
LN_EPS = 1e-5

VMEM_LIMIT_BYTES = 56 * 1024 * 1024

PROJ_TM = 1024
PROJ_TN = 1024
PROJ_SUB = 256
MIX_TM = 256
ATT_T = 256
ATT_HEADS = 4

_GELU_C = 0.7978845608028654
_LOG2_E = 1.4426950408889634


def _gelu_tanh(x):
    return 0.5 * x * (1.0 + jnp.tanh(_GELU_C * (x + 0.044715 * (x * x * x))))


def _silu(x):
    return 0.5 * x * (1.0 + jnp.tanh(0.5 * x))


def _layer_norm(z, g, b):
    mu = jnp.mean(z, axis=-1, keepdims=True)
    zc = z - mu
    var = jnp.mean(zc * zc, axis=-1, keepdims=True)
    return zc * lax.rsqrt(var + LN_EPS) * g + b


def _params(*semantics):
    return pltpu.CompilerParams(dimension_semantics=semantics, vmem_limit_bytes=VMEM_LIMIT_BYTES)


def _project_columns(x_ref, w_ref, o_ref, epilogue, b_ref=None):
    for n in range(w_ref.shape[1] // PROJ_SUB):
        cols = slice(n * PROJ_SUB, (n + 1) * PROJ_SUB)
        h = jnp.dot(x_ref[...], w_ref[:, cols], preferred_element_type=jnp.float32)
        if b_ref is not None:
            h = h + b_ref[:, cols]
        o_ref[:, cols] = epilogue(h).astype(o_ref.dtype)


def _gmlp_in_kernel(x_ref, w_ref, b_ref, o_ref, *, silu_from):
    j = pl.program_id(1)

    @pl.when(j < silu_from)
    def _():
        _project_columns(x_ref, w_ref, o_ref, _gelu_tanh, b_ref)

    @pl.when(j >= silu_from)
    def _():
        _project_columns(x_ref, w_ref, o_ref, _silu, b_ref)


def _gmlp_in(x, w, b):
    n, d = x.shape
    width = w.shape[1]
    grid = (n // PROJ_TM, width // PROJ_TN)
    return pl.pallas_call(
        functools.partial(_gmlp_in_kernel, silu_from=2 * A_WIDTH // PROJ_TN),
        grid=grid,
        in_specs=[
            pl.BlockSpec((PROJ_TM, d), lambda i, j: (i, 0)),
            pl.BlockSpec((d, PROJ_TN), lambda i, j: (0, j)),
            pl.BlockSpec((1, PROJ_TN), lambda i, j: (0, j)),
        ],
        out_specs=pl.BlockSpec((PROJ_TM, PROJ_TN), lambda i, j: (i, j)),
        out_shape=jax.ShapeDtypeStruct((n, width), jnp.bfloat16),
        compiler_params=_params("parallel", "arbitrary"),
        name="gmlp_in",
    )(x, w, b)


def _gmlp_out_kernel(u_ref, v_ref, g_ref, x_ref, vg_ref, vb_ref, ws_ref, bs_ref, wo_ref,
                     lg_ref, lb_ref, o_ref, ob_ref, s_ref):
    tm = u_ref.shape[0]
    vn = _layer_norm(v_ref[...].astype(jnp.float32), vg_ref[...], vb_ref[...]).astype(jnp.bfloat16)
    row = lax.broadcasted_iota(jnp.int32, (CHUNK, CHUNK), 0)
    col = lax.broadcasted_iota(jnp.int32, (CHUNK, CHUNK), 1)
    causal = col <= row
    for g in range(A_GROUPS):
        wt = jnp.where(causal, ws_ref[g], 0.0).astype(jnp.bfloat16)
        bias = bs_ref[g]
        cols = slice(g * A_GROUP_DIM, (g + 1) * A_GROUP_DIM)
        for c in range(tm // CHUNK):
            rows = slice(c * CHUNK, (c + 1) * CHUNK)
            mixed = jnp.dot(wt, vn[rows, cols], preferred_element_type=jnp.float32) + bias
            gate = u_ref[rows, cols].astype(jnp.float32) * g_ref[rows, cols].astype(jnp.float32)
            s_ref[rows, cols] = (gate * mixed).astype(jnp.bfloat16)
    y = jnp.dot(s_ref[...], wo_ref[...], preferred_element_type=jnp.float32)
    out = _layer_norm(DN_ALPHA * x_ref[...] + y, lg_ref[...], lb_ref[...])
    o_ref[...] = out
    ob_ref[...] = out.astype(jnp.bfloat16)


def _resident(shape):
    zeros = (0,) * len(shape)
    return pl.BlockSpec(shape, lambda i: zeros, pipeline_mode=pl.Buffered(1))


def _gmlp_out(hact, x, vln_g, vln_b, w_s, b_s, w_out, ln_g, ln_b):
    n, d = x.shape
    tm = MIX_TM
    return pl.pallas_call(
        _gmlp_out_kernel,
        grid=(n // tm,),
        in_specs=[
            pl.BlockSpec((tm, A_WIDTH), lambda i: (i, 0)),
            pl.BlockSpec((tm, A_WIDTH), lambda i: (i, 1)),
            pl.BlockSpec((tm, A_WIDTH), lambda i: (i, 2)),
            pl.BlockSpec((tm, d), lambda i: (i, 0)),
            _resident((1, A_WIDTH)),
            _resident((1, A_WIDTH)),
            _resident((A_GROUPS, CHUNK, CHUNK)),
            _resident((A_GROUPS, CHUNK, 1)),
            _resident((A_WIDTH, d)),
            _resident((1, d)),
            _resident((1, d)),
        ],
        out_specs=[
            pl.BlockSpec((tm, d), lambda i: (i, 0)),
            pl.BlockSpec((tm, d), lambda i: (i, 0)),
        ],
        out_shape=[
            jax.ShapeDtypeStruct((n, d), jnp.float32),
            jax.ShapeDtypeStruct((n, d), jnp.bfloat16),
        ],
        scratch_shapes=[pltpu.VMEM((tm, A_WIDTH), jnp.bfloat16)],
        compiler_params=_params("parallel"),
        name="gmlp_out",
    )(hact, hact, hact, x, vln_g, vln_b, w_s, b_s, w_out, ln_g, ln_b)


def _attn_in_kernel(x_ref, w_ref, o_ref, *, q_from, gate_from, q_scale):
    j = pl.program_id(1)

    @pl.when(j < q_from)
    def _():
        _project_columns(x_ref, w_ref, o_ref, lambda h: h)

    @pl.when(jnp.logical_and(j >= q_from, j < gate_from))
    def _():
        _project_columns(x_ref, w_ref, o_ref, lambda h: h * q_scale)

    @pl.when(j >= gate_from)
    def _():
        _project_columns(x_ref, w_ref, o_ref, _silu)


def _attn_in(xb, w):
    n, d = xb.shape
    width = w.shape[1]
    kernel = functools.partial(
        _attn_in_kernel,
        q_from=2 * B_WIDTH // PROJ_TN,
        gate_from=3 * B_WIDTH // PROJ_TN,
        q_scale=B_HEAD_DIM ** -0.5 * _LOG2_E,
    )
    return pl.pallas_call(
        kernel,
        grid=(n // PROJ_TM, width // PROJ_TN),
        in_specs=[
            pl.BlockSpec((PROJ_TM, d), lambda i, j: (i, 0)),
            pl.BlockSpec((d, PROJ_TN), lambda i, j: (0, j)),
        ],
        out_specs=pl.BlockSpec((PROJ_TM, PROJ_TN), lambda i, j: (i, j)),
        out_shape=jax.ShapeDtypeStruct((n, width), jnp.bfloat16),
        compiler_params=_params("parallel", "arbitrary"),
        name="attn_in",
    )(xb, w)


_EXP2_CLAMP = 64.0


def _softplus2(z):
    return jnp.maximum(z, jnp.log2(1.0 + jnp.exp2(jnp.minimum(z, _EXP2_CLAMP))))


def _attn_kernel(q_ref, k_ref, v_ref, m_ref, o_ref):
    t = ATT_T
    dh = B_HEAD_DIM
    qi = pl.program_id(2)
    neg_tri = m_ref[...]

    def sweep(kbs, carry, masked):
        heads = range(ATT_HEADS)
        starts = [pl.multiple_of(kb * t, t) for kb in kbs]
        chains = [(i, h) for i in range(len(kbs)) for h in heads]
        if masked:
            past = (lax.broadcasted_iota(jnp.int32, (t, t), 1)
                    < lax.broadcasted_iota(jnp.int32, (t, t), 0))
        z = {}
        for (i, h) in chains:
            z[i, h] = lax.dot_general(q_ref[:, h * dh:(h + 1) * dh],
                                      k_ref[pl.ds(starts[i], t), h * dh:(h + 1) * dh],
                                      (((1,), (1,)), ((), ())), preferred_element_type=jnp.float32)
        suffix = {}
        for (i, h) in chains:
            sp = _softplus2(z[i, h])
            if masked:
                sp = jnp.where(past, sp, 0.0)
            hi = sp.astype(jnp.bfloat16)
            lo = (sp - hi.astype(jnp.float32)).astype(jnp.bfloat16)
            suffix[i, h] = (jnp.dot(hi, neg_tri, preferred_element_type=jnp.float32)
                            + jnp.dot(lo, neg_tri, preferred_element_type=jnp.float32))
        cs = [carry[h][0] for h in heads]
        accs = [carry[h][1] for h in heads]
        for (i, h) in chains:
            w = jnp.exp2(z[i, h] + suffix[i, h] + cs[h])
            if masked:
                w = jnp.where(past, w, 0.0)
            accs[h] = accs[h] + jnp.dot(w.astype(jnp.bfloat16),
                                        v_ref[pl.ds(starts[i], t), h * dh:(h + 1) * dh],
                                        preferred_element_type=jnp.float32)
            cs[h] = cs[h] + suffix[i, h][:, 0:1]
        return tuple((cs[h], accs[h]) for h in heads)

    zero = (jnp.zeros((t, 1), jnp.float32), jnp.zeros((t, dh), jnp.float32))
    carry = sweep([qi], (zero,) * ATT_HEADS, True)
    odd = qi % 2
    carry = lax.cond(odd == 1, lambda c: sweep([qi - 1], c, False), lambda c: c, carry)
    top = qi - odd
    carry = lax.fori_loop(
        0, top // 2, lambda p, c: sweep([top - 1 - 2 * p, top - 2 - 2 * p], c, False), carry)
    for h in range(ATT_HEADS):
        o_ref[:, h * dh:(h + 1) * dh] = carry[h][1].astype(o_ref.dtype)


def _attention(proj, neg_tri, bsz, seq):
    t = ATT_T
    nq = seq // t
    hw = ATT_HEADS * B_HEAD_DIM
    nh = B_WIDTH // hw
    return pl.pallas_call(
        _attn_kernel,
        grid=(bsz, nh, nq),
        in_specs=[
            pl.BlockSpec((t, hw), lambda b, h, i: (b * nq + i, 2 * nh + h)),
            pl.BlockSpec((seq, hw), lambda b, h, i: (b, h)),
            pl.BlockSpec((seq, hw), lambda b, h, i: (b, nh + h)),
            pl.BlockSpec((t, t), lambda b, h, i: (0, 0)),
        ],
        out_specs=pl.BlockSpec((t, hw), lambda b, h, i: (b * nq + i, h)),
        out_shape=jax.ShapeDtypeStruct((bsz * seq, B_WIDTH), jnp.bfloat16),
        compiler_params=_params("parallel", "parallel", "arbitrary"),
        name="stick_breaking_attention",
    )(proj, proj, proj, neg_tri)


def _attn_out_kernel(o_ref, g_ref, x_ref, wo_ref, lg_ref, lb_ref, out_ref):
    s = (o_ref[...].astype(jnp.float32) * g_ref[...].astype(jnp.float32)).astype(jnp.bfloat16)
    y = jnp.dot(s, wo_ref[...], preferred_element_type=jnp.float32)
    out_ref[...] = _layer_norm(DN_ALPHA * x_ref[...] + y, lg_ref[...], lb_ref[...])


def _attn_out(o, proj, x, w_out, ln_g, ln_b):
    n, d = x.shape
    tm = MIX_TM
    return pl.pallas_call(
        _attn_out_kernel,
        grid=(n // tm,),
        in_specs=[
            pl.BlockSpec((tm, B_WIDTH), lambda i: (i, 0)),
            pl.BlockSpec((tm, B_WIDTH), lambda i: (i, 3)),
            pl.BlockSpec((tm, d), lambda i: (i, 0)),
            _resident((B_WIDTH, d)),
            _resident((1, d)),
            _resident((1, d)),
        ],
        out_specs=pl.BlockSpec((tm, d), lambda i: (i, 0)),
        out_shape=jax.ShapeDtypeStruct((n, d), jnp.float32),
        compiler_params=_params("parallel"),
        name="attn_out",
    )(o, proj, x, w_out, ln_g, ln_b)


def kernel(x, a_w_in, a_b_in, a_vln_g, a_vln_b, a_w_s, a_b_s, a_w_out, kv_w, b_w_in, b_w_out, ln_g, ln_b):
    bsz, seq, d = x.shape
    n = bsz * seq
    bf16 = jnp.bfloat16
    n_a = a_w_in.shape[0]
    n_b = b_w_in.shape[0]

    xf = x.reshape(n, d)
    xb = xf.astype(bf16)
    for i in range(n_a):
        hact = _gmlp_in(xb, a_w_in[i].astype(bf16), a_b_in[i][None, :])
        xf, xb = _gmlp_out(hact, xf, a_vln_g[i][None, :], a_vln_b[i][None, :], a_w_s[i],
                           a_b_s[i][:, :, None], a_w_out[i].astype(bf16),
                           ln_g[i][None, :], ln_b[i][None, :])

    idx = jnp.arange(ATT_T)
    neg_tri = jnp.where(idx[:, None] >= idx[None, :], -1.0, 0.0).astype(bf16)
    assert n_b == 1, "the fused k|v|q|gate projection supports exactly one attention layer"
    w_b = jnp.concatenate([kv_w.astype(bf16), b_w_in[0].astype(bf16)], axis=1)
    proj = _attn_in(xb, w_b)
    o = _attention(proj, neg_tri, bsz, seq)
    xf = _attn_out(o, proj, xf, b_w_out[0].astype(bf16), ln_g[n_a][None, :], ln_b[n_a][None, :])
    return xf.reshape(bsz, seq, d)
```

```python
import functools

import jax
import jax.numpy as jnp
from jax import lax
from jax.experimental import pallas as pl
from jax.experimental.pallas import tpu as pltpu

D_MODEL = 2048
DEPTH = 2
CHUNK = 128
A_WIDTH = 2 * D_MODEL
A_GROUPS = 8
A_GROUP_DIM = A_WIDTH // A_GROUPS
B_HEADS = 16
B_HEAD_DIM = D_MODEL // B_HEADS
B_WIDTH = B_HEADS * B_HEAD_DIM
DN_ALPHA = (2.0 * DEPTH) ** 0.25
LN_EPS = 1e-5

VMEM_LIMIT_BYTES = 56 * 1024 * 1024

PROJ_TM = 1024
PROJ_TN = 1024
PROJ_SUB = 256
MIX_TM = 256
ATT_T = 256
ATT_HEADS = 4

_GELU_C = 0.7978845608028654
_LOG2_E = 1.4426950408889634


def _gelu_tanh(x):
    inner = x * (_GELU_C + (_GELU_C * 0.044715) * (x * x))
    return (0.5 * x) * (1.0 + jnp.tanh(inner))


def _silu(x):
    half = 0.5 * x
    return half * (1.0 + jnp.tanh(half))


def _layer_norm(z, g, b):
    mu = jnp.mean(z, axis=-1, keepdims=True)
    zc = z - mu
    var = jnp.mean(zc * zc, axis=-1, keepdims=True)
    return zc * lax.rsqrt(var + LN_EPS) * g + b


def _params(*semantics):
    return pltpu.CompilerParams(dimension_semantics=semantics, vmem_limit_bytes=VMEM_LIMIT_BYTES)


def _project_columns(x_ref, w_ref, o_ref, epilogue, b_ref=None):
    for n in range(w_ref.shape[1] // PROJ_SUB):
        cols = slice(n * PROJ_SUB, (n + 1) * PROJ_SUB)
        h = jnp.dot(x_ref[...], w_ref[:, cols], preferred_element_type=jnp.float32)
        if b_ref is not None:
            h = h + b_ref[:, cols]
        o_ref[:, cols] = epilogue(h).astype(o_ref.dtype)


def _gmlp_in_kernel(x_ref, w_ref, b_ref, o_ref, xb_ref, *, silu_from):
    j = pl.program_id(1)

    @pl.when(j == 0)
    def _():
        xb_ref[...] = x_ref[...].astype(jnp.bfloat16)

    @pl.when(j < silu_from)
    def _():
        _project_columns(xb_ref, w_ref, o_ref, _gelu_tanh, b_ref)

    @pl.when(j >= silu_from)
    def _():
        _project_columns(xb_ref, w_ref, o_ref, _silu, b_ref)


def _gmlp_in(x, w, b):
    n, d = x.shape
    width = w.shape[1]
    grid = (n // PROJ_TM, width // PROJ_TN)
    return pl.pallas_call(
        functools.partial(_gmlp_in_kernel, silu_from=2 * A_WIDTH // PROJ_TN),
        grid=grid,
        in_specs=[
            pl.BlockSpec((PROJ_TM, d), lambda i, j: (i, 0)),
            pl.BlockSpec((d, PROJ_TN), lambda i, j: (0, j)),
            pl.BlockSpec((1, PROJ_TN), lambda i, j: (0, j)),
        ],
        out_specs=pl.BlockSpec((PROJ_TM, PROJ_TN), lambda i, j: (i, j)),
        out_shape=jax.ShapeDtypeStruct((n, width), jnp.bfloat16),
        scratch_shapes=[pltpu.VMEM((PROJ_TM, d), jnp.bfloat16)],
        compiler_params=_params("parallel", "arbitrary"),
        name="gmlp_in",
    )(x, w, b)


def _gmlp_out_kernel(u_ref, v_ref, g_ref, x_ref, vg_ref, vb_ref, ws_ref, bs_ref, wo_ref,
                     lg_ref, lb_ref, o_ref, ob_ref, s_ref):
    tm = u_ref.shape[0]
    vn = _layer_norm(v_ref[...].astype(jnp.float32), vg_ref[...], vb_ref[...]).astype(jnp.bfloat16)
    row = lax.broadcasted_iota(jnp.int32, (CHUNK, CHUNK), 0)
    col = lax.broadcasted_iota(jnp.int32, (CHUNK, CHUNK), 1)
    causal = col <= row
    for g in range(A_GROUPS):
        wt = jnp.where(causal, ws_ref[g], 0.0).astype(jnp.bfloat16)
        bias = bs_ref[g]
        cols = slice(g * A_GROUP_DIM, (g + 1) * A_GROUP_DIM)
        for c in range(tm // CHUNK):
            rows = slice(c * CHUNK, (c + 1) * CHUNK)
            mixed = jnp.dot(wt, vn[rows, cols], preferred_element_type=jnp.float32) + bias
            gate = u_ref[rows, cols].astype(jnp.float32) * g_ref[rows, cols].astype(jnp.float32)
            s_ref[rows, cols] = (gate * mixed).astype(jnp.bfloat16)
    y = jnp.dot(s_ref[...], wo_ref[...], preferred_element_type=jnp.float32)
    out = _layer_norm(DN_ALPHA * x_ref[...] + y, lg_ref[...], lb_ref[...])
    o_ref[...] = out
    ob_ref[...] = out.astype(jnp.bfloat16)


def _resident(shape):
    zeros = (0,) * len(shape)
    return pl.BlockSpec(shape, lambda i: zeros, pipeline_mode=pl.Buffered(1))


def _gmlp_out(hact, x, vln_g, vln_b, w_s, b_s, w_out, ln_g, ln_b):
    n, d = x.shape
    tm = MIX_TM
    return pl.pallas_call(
        _gmlp_out_kernel,
        grid=(n // tm,),
        in_specs=[
            pl.BlockSpec((tm, A_WIDTH), lambda i: (i, 0)),
            pl.BlockSpec((tm, A_WIDTH), lambda i: (i, 1)),
            pl.BlockSpec((tm, A_WIDTH), lambda i: (i, 2)),
            pl.BlockSpec((tm, d), lambda i: (i, 0)),
            _resident((1, A_WIDTH)),
            _resident((1, A_WIDTH)),
            _resident((A_GROUPS, CHUNK, CHUNK)),
            _resident((A_GROUPS, CHUNK, 1)),
            _resident((A_WIDTH, d)),
            _resident((1, d)),
            _resident((1, d)),
        ],
        out_specs=[
            pl.BlockSpec((tm, d), lambda i: (i, 0)),
            pl.BlockSpec((tm, d), lambda i: (i, 0)),
        ],
        out_shape=[
            jax.ShapeDtypeStruct((n, d), jnp.float32),
            jax.ShapeDtypeStruct((n, d), jnp.bfloat16),
        ],
        scratch_shapes=[pltpu.VMEM((tm, A_WIDTH), jnp.bfloat16)],
        compiler_params=_params("parallel"),
        name="gmlp_out",
    )(hact, hact, hact, x, vln_g, vln_b, w_s, b_s, w_out, ln_g, ln_b)


def _attn_in_kernel(x_ref, w_ref, o_ref, *, q_from, gate_from, q_scale):
    j = pl.program_id(1)

    @pl.when(j < q_from)
    def _():
        _project_columns(x_ref, w_ref, o_ref, lambda h: h)

    @pl.when(jnp.logical_and(j >= q_from, j < gate_from))
    def _():
        _project_columns(x_ref, w_ref, o_ref, lambda h: h * q_scale)

    @pl.when(j >= gate_from)
    def _():
        _project_columns(x_ref, w_ref, o_ref, _silu)


def _attn_in(xb, w):
    n, d = xb.shape
    width = w.shape[1]
    kernel = functools.partial(
        _attn_in_kernel,
        q_from=2 * B_WIDTH // PROJ_TN,
        gate_from=3 * B_WIDTH // PROJ_TN,
        q_scale=B_HEAD_DIM ** -0.5 * _LOG2_E,
    )
    return pl.pallas_call(
        kernel,
        grid=(n // PROJ_TM, width // PROJ_TN),
        in_specs=[
            pl.BlockSpec((PROJ_TM, d), lambda i, j: (i, 0)),
            pl.BlockSpec((d, PROJ_TN), lambda i, j: (0, j)),
        ],
        out_specs=pl.BlockSpec((PROJ_TM, PROJ_TN), lambda i, j: (i, j)),
        out_shape=jax.ShapeDtypeStruct((n, width), jnp.bfloat16),
        compiler_params=_params("parallel", "arbitrary"),
        name="attn_in",
    )(xb, w)


_EXP2_CLAMP = 64.0


def _softplus2(z):
    return jnp.maximum(z, jnp.log2(1.0 + jnp.exp2(jnp.minimum(z, _EXP2_CLAMP))))


def _attn_kernel(q_ref, k_ref, v_ref, m_ref, o_ref):
    t = ATT_T
    dh = B_HEAD_DIM
    qi = pl.program_id(2)
    neg_tri = m_ref[...]

    def sweep(kbs, carry, masked):
        heads = range(ATT_HEADS)
        starts = [pl.multiple_of(kb * t, t) for kb in kbs]
        chains = [(i, h) for i in range(len(kbs)) for h in heads]
        if masked:
            past = (lax.broadcasted_iota(jnp.int32, (t, t), 1)
                    < lax.broadcasted_iota(jnp.int32, (t, t), 0))
        z = {}
        for (i, h) in chains:
            z[i, h] = lax.dot_general(q_ref[:, h * dh:(h + 1) * dh],
                                      k_ref[pl.ds(starts[i], t), h * dh:(h + 1) * dh],
                                      (((1,), (1,)), ((), ())), preferred_element_type=jnp.float32)
        suffix = {}
        for (i, h) in chains:
            sp = _softplus2(z[i, h])
            if masked:
                sp = jnp.where(past, sp, 0.0)
            suffix[i, h] = jnp.dot(sp.astype(jnp.bfloat16), neg_tri,
                                   preferred_element_type=jnp.float32)
        cs = [carry[h][0] for h in heads]
        accs = [carry[h][1] for h in heads]
        for (i, h) in chains:
            w = jnp.exp2(z[i, h] + suffix[i, h] + cs[h])
            if masked:
                w = jnp.where(past, w, 0.0)
            accs[h] = accs[h] + jnp.dot(w.astype(jnp.bfloat16),
                                        v_ref[pl.ds(starts[i], t), h * dh:(h + 1) * dh],
                                        preferred_element_type=jnp.float32)
            cs[h] = cs[h] + suffix[i, h][:, 0:1]
        return tuple((cs[h], accs[h]) for h in heads)

    zero = (jnp.zeros((t, 1), jnp.float32), jnp.zeros((t, dh), jnp.float32))
    carry = sweep([qi], (zero,) * ATT_HEADS, True)
    odd = qi % 2
    carry = lax.cond(odd == 1, lambda c: sweep([qi - 1], c, False), lambda c: c, carry)
    top = qi - odd
    carry = lax.fori_loop(
        0, top // 2, lambda p, c: sweep([top - 1 - 2 * p, top - 2 - 2 * p], c, False), carry)
    for h in range(ATT_HEADS):
        o_ref[:, h * dh:(h + 1) * dh] = carry[h][1].astype(o_ref.dtype)


def _attention(proj, neg_tri, bsz, seq):
    t = ATT_T
    nq = seq // t
    hw = ATT_HEADS * B_HEAD_DIM
    nh = B_WIDTH // hw
    return pl.pallas_call(
        _attn_kernel,
        grid=(bsz, nh, nq),
        in_specs=[
            pl.BlockSpec((t, hw), lambda b, h, i: (b * nq + i, 2 * nh + h)),
            pl.BlockSpec((seq, hw), lambda b, h, i: (b, h)),
            pl.BlockSpec((seq, hw), lambda b, h, i: (b, nh + h)),
            pl.BlockSpec((t, t), lambda b, h, i: (0, 0)),
        ],
        out_specs=pl.BlockSpec((t, hw), lambda b, h, i: (b * nq + i, h)),
        out_shape=jax.ShapeDtypeStruct((bsz * seq, B_WIDTH), jnp.bfloat16),
        compiler_params=_params("parallel", "parallel", "arbitrary"),
        name="stick_breaking_attention",
    )(proj, proj, proj, neg_tri)


def _attn_out_kernel(o_ref, g_ref, x_ref, wo_ref, lg_ref, lb_ref, out_ref):
    s = (o_ref[...].astype(jnp.float32) * g_ref[...].astype(jnp.float32)).astype(jnp.bfloat16)
    y = jnp.dot(s, wo_ref[...], preferred_element_type=jnp.float32)
    out_ref[...] = _layer_norm(DN_ALPHA * x_ref[...] + y, lg_ref[...], lb_ref[...])


def _attn_out(o, proj, x, w_out, ln_g, ln_b):
    n, d = x.shape
    tm = MIX_TM
    return pl.pallas_call(
        _attn_out_kernel,
        grid=(n // tm,),
        in_specs=[
            pl.BlockSpec((tm, B_WIDTH), lambda i: (i, 0)),
            pl.BlockSpec((tm, B_WIDTH), lambda i: (i, 3)),
            pl.BlockSpec((tm, d), lambda i: (i, 0)),
            _resident((B_WIDTH, d)),
            _resident((1, d)),
            _resident((1, d)),
        ],
        out_specs=pl.BlockSpec((tm, d), lambda i: (i, 0)),
        out_shape=jax.ShapeDtypeStruct((n, d), jnp.float32),
        compiler_params=_params("parallel"),
        name="attn_out",
    )(o, proj, x, w_out, ln_g, ln_b)


def kernel(x, a_w_in, a_b_in, a_vln_g, a_vln_b, a_w_s, a_b_s, a_w_out, kv_w, b_w_in, b_w_out, ln_g, ln_b):
    bsz, seq, d = x.shape
    n = bsz * seq
    bf16 = jnp.bfloat16
    n_a = a_w_in.shape[0]
    n_b = b_w_in.shape[0]

    assert n_a >= 1, "the attention layers read the bf16 stream written by the last gMLP layer"
    xf = x.reshape(n, d)
    xb = None
    for i in range(n_a):
        hact = _gmlp_in(xf, a_w_in[i].astype(bf16), a_b_in[i][None, :])
        xf, xb = _gmlp_out(hact, xf, a_vln_g[i][None, :], a_vln_b[i][None, :], a_w_s[i],
                           a_b_s[i][:, :, None], a_w_out[i].astype(bf16),
                           ln_g[i][None, :], ln_b[i][None, :])

    idx = jnp.arange(ATT_T)
    neg_tri = jnp.where(idx[:, None] >= idx[None, :], -1.0, 0.0).astype(bf16)
    assert n_b == 1, "the fused k|v|q|gate projection supports exactly one attention layer"
    w_b = jnp.concatenate([kv_w.astype(bf16), b_w_in[0].astype(bf16)], axis=1)
    proj = _attn_in(xb, w_b)
    o = _attention(proj, neg_tri, bsz, seq)
    xf = _attn_out(o, proj, xf, b_w_out[0].astype(bf16), ln_g[n_a][None, :], ln_b[n_a][None, :])
    return xf.reshape(bsz, seq, d)
```

```python
import functools

import jax
import jax.numpy as jnp
from jax import lax
from jax.experimental import pallas as pl
from jax.experimental.pallas import tpu as pltpu

D_MODEL = 2048
DEPTH = 2
CHUNK = 128
A_WIDTH = 2 * D_MODEL
A_GROUPS = 8
A_GROUP_DIM = A_WIDTH // A_GROUPS
B_HEADS = 16
B_HEAD_DIM = D_MODEL // B_HEADS
B_WIDTH = B_HEADS * B_HEAD_DIM
DN_ALPHA = (2.0 * DEPTH) ** 0.25
LN_EPS = 1e-5

VMEM_LIMIT_BYTES = 56 * 1024 * 1024

PROJ_TM = 1024
PROJ_TN = 1024
PROJ_SUB = 256
MIX_TM = 256
ATT_T = 256
ATT_HEADS = 8
ATT_LAG = 2

_GELU_C = 0.7978845608028654
_LOG2_E = 1.4426950408889634


def _gelu_tanh(x):
    inner = x * (_GELU_C + (_GELU_C * 0.044715) * (x * x))
    return (0.5 * x) * (1.0 + jnp.tanh(inner))


def _silu(x):
    half = 0.5 * x
    return half * (1.0 + jnp.tanh(half))


def _layer_norm(z, g, b):
    mu = jnp.mean(z, axis=-1, keepdims=True)
    zc = z - mu
    var = jnp.mean(zc * zc, axis=-1, keepdims=True)
    return zc * lax.rsqrt(var + LN_EPS) * g + b


def _params(*semantics):
    return pltpu.CompilerParams(dimension_semantics=semantics, vmem_limit_bytes=VMEM_LIMIT_BYTES)


def _project_columns(x_ref, w_ref, o_ref, epilogue, b_ref=None):
    for n in range(w_ref.shape[1] // PROJ_SUB):
        cols = slice(n * PROJ_SUB, (n + 1) * PROJ_SUB)
        h = jnp.dot(x_ref[...], w_ref[:, cols].astype(jnp.bfloat16),
                    preferred_element_type=jnp.float32)
        if b_ref is not None:
            h = h + b_ref[:, cols]
        o_ref[:, cols] = epilogue(h).astype(o_ref.dtype)


def _gmlp_in_kernel(x_ref, w_ref, b_ref, o_ref, xb_ref, *, silu_from):
    j = pl.program_id(1)

    @pl.when(j == 0)
    def _():
        xb_ref[...] = x_ref[...].astype(jnp.bfloat16)

    @pl.when(j < silu_from)
    def _():
        _project_columns(xb_ref, w_ref, o_ref, _gelu_tanh, b_ref)

    @pl.when(j >= silu_from)
    def _():
        _project_columns(xb_ref, w_ref, o_ref, _silu, b_ref)


def _gmlp_in(x, w, b):
    n, d = x.shape
    width = w.shape[1]
    grid = (n // PROJ_TM, width // PROJ_TN)
    return pl.pallas_call(
        functools.partial(_gmlp_in_kernel, silu_from=2 * A_WIDTH // PROJ_TN),
        grid=grid,
        in_specs=[
            pl.BlockSpec((PROJ_TM, d), lambda i, j: (i, 0)),
            pl.BlockSpec((d, PROJ_TN), lambda i, j: (0, j)),
            pl.BlockSpec((1, PROJ_TN), lambda i, j: (0, j)),
        ],
        out_specs=pl.BlockSpec((PROJ_TM, PROJ_TN), lambda i, j: (i, j)),
        out_shape=jax.ShapeDtypeStruct((n, width), jnp.bfloat16),
        scratch_shapes=[pltpu.VMEM((PROJ_TM, d), jnp.bfloat16)],
        compiler_params=_params("parallel", "arbitrary"),
        name="gmlp_in",
    )(x, w, b)


def _gmlp_out_kernel(u_ref, v_ref, g_ref, x_ref, vg_ref, vb_ref, ws_ref, bs_ref, wo_ref,
                     lg_ref, lb_ref, o_ref, ob_ref, s_ref):
    tm = u_ref.shape[0]
    vn = _layer_norm(v_ref[...].astype(jnp.float32), vg_ref[...], vb_ref[...]).astype(jnp.bfloat16)
    row = lax.broadcasted_iota(jnp.int32, (CHUNK, CHUNK), 0)
    col = lax.broadcasted_iota(jnp.int32, (CHUNK, CHUNK), 1)
    causal = col <= row
    for g in range(A_GROUPS):
        wt = jnp.where(causal, ws_ref[g], 0.0).astype(jnp.bfloat16)
        bias = bs_ref[g]
        cols = slice(g * A_GROUP_DIM, (g + 1) * A_GROUP_DIM)
        for c in range(tm // CHUNK):
            rows = slice(c * CHUNK, (c + 1) * CHUNK)
            mixed = jnp.dot(wt, vn[rows, cols], preferred_element_type=jnp.float32) + bias
            gate = u_ref[rows, cols].astype(jnp.float32) * g_ref[rows, cols].astype(jnp.float32)
            s_ref[rows, cols] = (gate * mixed).astype(jnp.bfloat16)
    y = jnp.dot(s_ref[...], wo_ref[...], preferred_element_type=jnp.float32)
    out = _layer_norm(DN_ALPHA * x_ref[...] + y, lg_ref[...], lb_ref[...])
    o_ref[...] = out
    ob_ref[...] = out.astype(jnp.bfloat16)


def _resident(shape):
    zeros = (0,) * len(shape)
    return pl.BlockSpec(shape, lambda i: zeros, pipeline_mode=pl.Buffered(1))


def _gmlp_out(hact, x, vln_g, vln_b, w_s, b_s, w_out, ln_g, ln_b):
    n, d = x.shape
    tm = MIX_TM
    return pl.pallas_call(
        _gmlp_out_kernel,
        grid=(n // tm,),
        in_specs=[
            pl.BlockSpec((tm, A_WIDTH), lambda i: (i, 0)),
            pl.BlockSpec((tm, A_WIDTH), lambda i: (i, 1)),
            pl.BlockSpec((tm, A_WIDTH), lambda i: (i, 2)),
            pl.BlockSpec((tm, d), lambda i: (i, 0)),
            _resident((1, A_WIDTH)),
            _resident((1, A_WIDTH)),
            _resident((A_GROUPS, CHUNK, CHUNK)),
            _resident((A_GROUPS, CHUNK, 1)),
            _resident((A_WIDTH, d)),
            _resident((1, d)),
            _resident((1, d)),
        ],
        out_specs=[
            pl.BlockSpec((tm, d), lambda i: (i, 0)),
            pl.BlockSpec((tm, d), lambda i: (i, 0)),
        ],
        out_shape=[
            jax.ShapeDtypeStruct((n, d), jnp.float32),
            jax.ShapeDtypeStruct((n, d), jnp.bfloat16),
        ],
        scratch_shapes=[pltpu.VMEM((tm, A_WIDTH), jnp.bfloat16)],
        compiler_params=_params("parallel"),
        name="gmlp_out",
    )(hact, hact, hact, x, vln_g, vln_b, w_s, b_s, w_out, ln_g, ln_b)


def _attn_in_kernel(x_ref, w_ref, o_ref, *, q_from, gate_from, q_scale):
    j = pl.program_id(1)

    @pl.when(j < q_from)
    def _():
        _project_columns(x_ref, w_ref, o_ref, lambda h: h)

    @pl.when(jnp.logical_and(j >= q_from, j < gate_from))
    def _():
        _project_columns(x_ref, w_ref, o_ref, lambda h: h * q_scale)

    @pl.when(j >= gate_from)
    def _():
        _project_columns(x_ref, w_ref, o_ref, _silu)


def _attn_in(xb, w):
    n, d = xb.shape
    width = w.shape[1]
    kernel = functools.partial(
        _attn_in_kernel,
        q_from=2 * B_WIDTH // PROJ_TN,
        gate_from=3 * B_WIDTH // PROJ_TN,
        q_scale=B_HEAD_DIM ** -0.5 * _LOG2_E,
    )
    return pl.pallas_call(
        kernel,
        grid=(n // PROJ_TM, width // PROJ_TN),
        in_specs=[
            pl.BlockSpec((PROJ_TM, d), lambda i, j: (i, 0)),
            pl.BlockSpec((d, PROJ_TN), lambda i, j: (0, j)),
        ],
        out_specs=pl.BlockSpec((PROJ_TM, PROJ_TN), lambda i, j: (i, j)),
        out_shape=jax.ShapeDtypeStruct((n, width), jnp.bfloat16),
        compiler_params=_params("parallel", "arbitrary"),
        name="attn_in",
    )(xb, w)


_EXP2_CLAMP = 64.0


def _softplus2(z):
    return jnp.maximum(z, jnp.log2(1.0 + jnp.exp2(jnp.minimum(z, _EXP2_CLAMP))))


def _attn_kernel(q_ref, k_ref, v_ref, m_ref, o_ref):
    t = ATT_T
    dh = B_HEAD_DIM
    qi = pl.program_id(2)
    neg_tri = m_ref[...]

    def sweep(kbs, diagonal, carry):
        heads = range(ATT_HEADS)
        starts = [pl.multiple_of(kb * t, t) for kb in kbs]
        chains = [(i, h) for i in range(len(kbs)) for h in heads]
        if any(diagonal):
            past = (lax.broadcasted_iota(jnp.int32, (t, t), 1)
                    < lax.broadcasted_iota(jnp.int32, (t, t), 0))
        z, suffix = {}, {}
        cs = [carry[h][0] for h in heads]
        accs = [carry[h][1] for h in heads]

        def scores(i, h):
            z[i, h] = lax.dot_general(q_ref[:, h * dh:(h + 1) * dh],
                                      k_ref[pl.ds(starts[i], t), h * dh:(h + 1) * dh],
                                      (((1,), (1,)), ((), ())), preferred_element_type=jnp.float32)

        def suffix_sum(i, h):
            sp = _softplus2(z[i, h])
            if diagonal[i]:
                sp = jnp.where(past, sp, 0.0)
            suffix[i, h] = jnp.dot(sp.astype(jnp.bfloat16), neg_tri,
                                   preferred_element_type=jnp.float32)

        def values(i, h):
            w = jnp.exp2(z[i, h] + suffix[i, h] + cs[h])
            if diagonal[i]:
                w = jnp.where(past, w, 0.0)
            accs[h] = accs[h] + jnp.dot(w.astype(jnp.bfloat16),
                                        v_ref[pl.ds(starts[i], t), h * dh:(h + 1) * dh],
                                        preferred_element_type=jnp.float32)
            cs[h] = cs[h] + suffix[i, h][:, 0:1]

        n = len(chains)
        for step in range(n + 2 * ATT_LAG):
            if 0 <= step - ATT_LAG < n:
                suffix_sum(*chains[step - ATT_LAG])
            if step < n:
                scores(*chains[step])
            if 0 <= step - 2 * ATT_LAG < n:
                values(*chains[step - 2 * ATT_LAG])
        return tuple((cs[h], accs[h]) for h in heads)

    zero = (jnp.zeros((t, 1), jnp.float32), jnp.zeros((t, dh), jnp.float32))
    carry = (zero,) * ATT_HEADS
    paired = qi % 2
    carry = lax.cond(paired == 1,
                     lambda c: sweep([qi, qi - 1], [True, False], c),
                     lambda c: sweep([qi], [True], c), carry)
    top = qi - paired
    carry = lax.fori_loop(
        0, top // 2,
        lambda p, c: sweep([top - 1 - 2 * p, top - 2 - 2 * p], [False, False], c), carry)
    for h in range(ATT_HEADS):
        o_ref[:, h * dh:(h + 1) * dh] = carry[h][1].astype(o_ref.dtype)


def _attention(proj, neg_tri, bsz, seq):
    t = ATT_T
    nq = seq // t
    hw = ATT_HEADS * B_HEAD_DIM
    nh = B_WIDTH // hw
    return pl.pallas_call(
        _attn_kernel,
        grid=(bsz, nh, nq),
        in_specs=[
            pl.BlockSpec((t, hw), lambda b, h, i: (b * nq + i, 2 * nh + h)),
            pl.BlockSpec((seq, hw), lambda b, h, i: (b, h)),
            pl.BlockSpec((seq, hw), lambda b, h, i: (b, nh + h)),
            pl.BlockSpec((t, t), lambda b, h, i: (0, 0)),
        ],
        out_specs=pl.BlockSpec((t, hw), lambda b, h, i: (b * nq + i, h)),
        out_shape=jax.ShapeDtypeStruct((bsz * seq, B_WIDTH), jnp.bfloat16),
        compiler_params=_params("parallel", "parallel", "arbitrary"),
        name="stick_breaking_attention",
    )(proj, proj, proj, neg_tri)


def _attn_out_kernel(o_ref, g_ref, x_ref, wo_ref, lg_ref, lb_ref, out_ref):
    s = (o_ref[...].astype(jnp.float32) * g_ref[...].astype(jnp.float32)).astype(jnp.bfloat16)
    y = jnp.dot(s, wo_ref[...], preferred_element_type=jnp.float32)
    out_ref[...] = _layer_norm(DN_ALPHA * x_ref[...] + y, lg_ref[...], lb_ref[...])


def _attn_out(o, proj, x, w_out, ln_g, ln_b):
    n, d = x.shape
    tm = MIX_TM
    return pl.pallas_call(
        _attn_out_kernel,
        grid=(n // tm,),
        in_specs=[
            pl.BlockSpec((tm, B_WIDTH), lambda i: (i, 0)),
            pl.BlockSpec((tm, B_WIDTH), lambda i: (i, 3)),
            pl.BlockSpec((tm, d), lambda i: (i, 0)),
            _resident((B_WIDTH, d)),
            _resident((1, d)),
            _resident((1, d)),
        ],
        out_specs=pl.BlockSpec((tm, d), lambda i: (i, 0)),
        out_shape=jax.ShapeDtypeStruct((n, d), jnp.float32),
        compiler_params=_params("parallel"),
        name="attn_out",
    )(o, proj, x, w_out, ln_g, ln_b)


def kernel(x, a_w_in, a_b_in, a_vln_g, a_vln_b, a_w_s, a_b_s, a_w_out, kv_w, b_w_in, b_w_out, ln_g, ln_b):
    bsz, seq, d = x.shape
    n = bsz * seq
    bf16 = jnp.bfloat16
    n_a = a_w_in.shape[0]
    n_b = b_w_in.shape[0]

    assert n_a >= 1, "the attention layers read the bf16 stream written by the last gMLP layer"
    xf = x.reshape(n, d)
    xb = None
    for i in range(n_a):
        hact = _gmlp_in(xf, a_w_in[i], a_b_in[i][None, :])
        xf, xb = _gmlp_out(hact, xf, a_vln_g[i][None, :], a_vln_b[i][None, :], a_w_s[i],
                           a_b_s[i][:, :, None], a_w_out[i].astype(bf16),
                           ln_g[i][None, :], ln_b[i][None, :])

    idx = jnp.arange(ATT_T)
    neg_tri = jnp.where(idx[:, None] >= idx[None, :], -1.0, 0.0).astype(bf16)
    assert n_b == 1, "the fused k|v|q|gate projection supports exactly one attention layer"
    w_b = jnp.concatenate([kv_w.astype(bf16), b_w_in[0].astype(bf16)], axis=1)
    proj = _attn_in(xb, w_b)
    o = _attention(proj, neg_tri, bsz, seq)
    xf = _attn_out(o, proj, xf, b_w_out[0].astype(bf16), ln_g[n_a][None, :], ln_b[n_a][None, :])
    return xf.reshape(bsz, seq, d)
```

```python
import functools

import jax
import jax.numpy as jnp
from jax import lax
from jax.experimental import pallas as pl
from jax.experimental.pallas import tpu as pltpu

D_MODEL = 2048
DEPTH = 2
CHUNK = 128
A_WIDTH = 2 * D_MODEL
A_GROUPS = 8
A_GROUP_DIM = A_WIDTH // A_GROUPS
B_HEADS = 16
B_HEAD_DIM = D_MODEL // B_HEADS
B_WIDTH = B_HEADS * B_HEAD_DIM
DN_ALPHA = (2.0 * DEPTH) ** 0.25
LN_EPS = 1e-5

VMEM_LIMIT_BYTES = 56 * 1024 * 1024

PROJ_TM = 1024
PROJ_TN = 1024
PROJ_SUB = 256
PROJ_SLOTS = 3
MIX_TM = 256
ATT_T = 256
ATT_HEADS = 8
ATT_LAG = 2

_GELU_C = 0.7978845608028654
_LOG2_E = 1.4426950408889634


def _gelu_tanh(x):
    inner = x * (_GELU_C + (_GELU_C * 0.044715) * (x * x))
    return (0.5 * x) * (1.0 + jnp.tanh(inner))


def _silu(x):
    half = 0.5 * x
    return half * (1.0 + jnp.tanh(half))


def _layer_norm(z, g, b):
    mu = jnp.mean(z, axis=-1, keepdims=True)
    zc = z - mu
    var = jnp.mean(zc * zc, axis=-1, keepdims=True)
    return zc * lax.rsqrt(var + LN_EPS) * g + b


def _params(*semantics):
    return pltpu.CompilerParams(dimension_semantics=semantics, vmem_limit_bytes=VMEM_LIMIT_BYTES)


def _project_columns(x_ref, w_ref, o_ref, acc_ref, epilogue, b_ref=None):
    groups = w_ref.shape[1] // PROJ_SUB

    def product(n):
        cols = slice(n * PROJ_SUB, (n + 1) * PROJ_SUB)
        acc_ref[n % PROJ_SLOTS] = jnp.dot(x_ref[...], w_ref[:, cols].astype(jnp.bfloat16),
                                          preferred_element_type=jnp.float32)

    def finish(m):
        cols = slice(m * PROJ_SUB, (m + 1) * PROJ_SUB)
        h = acc_ref[m % PROJ_SLOTS]
        if b_ref is not None:
            h = h + b_ref[:, cols]
        o_ref[:, cols] = epilogue(h).astype(o_ref.dtype)

    lag = PROJ_SLOTS - 1
    for n in range(groups + lag):
        if n < groups:
            product(n)
        if n >= lag:
            finish(n - lag)


def _proj_scratch():
    return pltpu.VMEM((PROJ_SLOTS, PROJ_TM, PROJ_SUB), jnp.float32)


def _gmlp_in_kernel(x_ref, w_ref, b_ref, o_ref, xb_ref, acc_ref, *, silu_from):
    j = pl.program_id(1)

    @pl.when(j == 0)
    def _():
        xb_ref[...] = x_ref[...].astype(jnp.bfloat16)

    @pl.when(j < silu_from)
    def _():
        _project_columns(xb_ref, w_ref, o_ref, acc_ref, _gelu_tanh, b_ref)

    @pl.when(j >= silu_from)
    def _():
        _project_columns(xb_ref, w_ref, o_ref, acc_ref, _silu, b_ref)


def _gmlp_in(x, w, b):
    n, d = x.shape
    width = w.shape[1]
    grid = (n // PROJ_TM, width // PROJ_TN)
    return pl.pallas_call(
        functools.partial(_gmlp_in_kernel, silu_from=2 * A_WIDTH // PROJ_TN),
        grid=grid,
        in_specs=[
            pl.BlockSpec((PROJ_TM, d), lambda i, j: (i, 0)),
            pl.BlockSpec((d, PROJ_TN), lambda i, j: (0, j)),
            pl.BlockSpec((1, PROJ_TN), lambda i, j: (0, j)),
        ],
        out_specs=pl.BlockSpec((PROJ_TM, PROJ_TN), lambda i, j: (i, j)),
        out_shape=jax.ShapeDtypeStruct((n, width), jnp.bfloat16),
        scratch_shapes=[pltpu.VMEM((PROJ_TM, d), jnp.bfloat16), _proj_scratch()],
        compiler_params=_params("parallel", "arbitrary"),
        name="gmlp_in",
    )(x, w, b)


def _gmlp_out_kernel(u_ref, v_ref, g_ref, x_ref, vg_ref, vb_ref, ws_ref, bs_ref, wo_ref,
                     lg_ref, lb_ref, o_ref, ob_ref, s_ref):
    tm = u_ref.shape[0]
    vn = _layer_norm(v_ref[...].astype(jnp.float32), vg_ref[...], vb_ref[...]).astype(jnp.bfloat16)
    row = lax.broadcasted_iota(jnp.int32, (CHUNK, CHUNK), 0)
    col = lax.broadcasted_iota(jnp.int32, (CHUNK, CHUNK), 1)
    causal = col <= row
    for g in range(A_GROUPS):
        wt = jnp.where(causal, ws_ref[g], 0.0).astype(jnp.bfloat16)
        bias = bs_ref[g]
        cols = slice(g * A_GROUP_DIM, (g + 1) * A_GROUP_DIM)
        for c in range(tm // CHUNK):
            rows = slice(c * CHUNK, (c + 1) * CHUNK)
            mixed = jnp.dot(wt, vn[rows, cols], preferred_element_type=jnp.float32) + bias
            gate = u_ref[rows, cols].astype(jnp.float32) * g_ref[rows, cols].astype(jnp.float32)
            s_ref[rows, cols] = (gate * mixed).astype(jnp.bfloat16)
    y = jnp.dot(s_ref[...], wo_ref[...], preferred_element_type=jnp.float32)
    out = _layer_norm(DN_ALPHA * x_ref[...] + y, lg_ref[...], lb_ref[...])
    o_ref[...] = out
    ob_ref[...] = out.astype(jnp.bfloat16)


def _resident(shape):
    zeros = (0,) * len(shape)
    return pl.BlockSpec(shape, lambda i: zeros, pipeline_mode=pl.Buffered(1))


def _gmlp_out(hact, x, vln_g, vln_b, w_s, b_s, w_out, ln_g, ln_b):
    n, d = x.shape
    tm = MIX_TM
    return pl.pallas_call(
        _gmlp_out_kernel,
        grid=(n // tm,),
        in_specs=[
            pl.BlockSpec((tm, A_WIDTH), lambda i: (i, 0)),
            pl.BlockSpec((tm, A_WIDTH), lambda i: (i, 1)),
            pl.BlockSpec((tm, A_WIDTH), lambda i: (i, 2)),
            pl.BlockSpec((tm, d), lambda i: (i, 0)),
            _resident((1, A_WIDTH)),
            _resident((1, A_WIDTH)),
            _resident((A_GROUPS, CHUNK, CHUNK)),
            _resident((A_GROUPS, CHUNK, 1)),
            _resident((A_WIDTH, d)),
            _resident((1, d)),
            _resident((1, d)),
        ],
        out_specs=[
            pl.BlockSpec((tm, d), lambda i: (i, 0)),
            pl.BlockSpec((tm, d), lambda i: (i, 0)),
        ],
        out_shape=[
            jax.ShapeDtypeStruct((n, d), jnp.float32),
            jax.ShapeDtypeStruct((n, d), jnp.bfloat16),
        ],
        scratch_shapes=[pltpu.VMEM((tm, A_WIDTH), jnp.bfloat16)],
        compiler_params=_params("parallel"),
        name="gmlp_out",
    )(hact, hact, hact, x, vln_g, vln_b, w_s, b_s, w_out, ln_g, ln_b)


def _attn_in_kernel(x_ref, wkv_ref, wqg_ref, o_ref, acc_ref, *, q_from, gate_from, q_scale):
    j = pl.program_id(1)

    @pl.when(j < q_from)
    def _():
        _project_columns(x_ref, wkv_ref, o_ref, acc_ref, lambda h: h)

    @pl.when(jnp.logical_and(j >= q_from, j < gate_from))
    def _():
        _project_columns(x_ref, wqg_ref, o_ref, acc_ref, lambda h: h * q_scale)

    @pl.when(j >= gate_from)
    def _():
        _project_columns(x_ref, wqg_ref, o_ref, acc_ref, _silu)


def _attn_in(xb, w_kv, w_qg):
    n, d = xb.shape
    kv_tiles = w_kv.shape[1] // PROJ_TN
    qg_tiles = w_qg.shape[1] // PROJ_TN
    kernel = functools.partial(
        _attn_in_kernel,
        q_from=kv_tiles,
        gate_from=kv_tiles + qg_tiles // 2,
        q_scale=B_HEAD_DIM ** -0.5 * _LOG2_E,
    )
    return pl.pallas_call(
        kernel,
        grid=(n // PROJ_TM, kv_tiles + qg_tiles),
        in_specs=[
            pl.BlockSpec((PROJ_TM, d), lambda i, j: (i, 0)),
            pl.BlockSpec((d, PROJ_TN), lambda i, j: (0, jnp.minimum(j, kv_tiles - 1))),
            pl.BlockSpec((d, PROJ_TN), lambda i, j: (0, jnp.maximum(j - kv_tiles, 0))),
        ],
        out_specs=pl.BlockSpec((PROJ_TM, PROJ_TN), lambda i, j: (i, j)),
        out_shape=jax.ShapeDtypeStruct((n, w_kv.shape[1] + w_qg.shape[1]), jnp.bfloat16),
        scratch_shapes=[_proj_scratch()],
        compiler_params=_params("parallel", "arbitrary"),
        name="attn_in",
    )(xb, w_kv, w_qg)


_EXP2_CLAMP = 64.0


def _softplus2(z):
    return jnp.maximum(z, jnp.log2(1.0 + jnp.exp2(jnp.minimum(z, _EXP2_CLAMP))))


def _attn_kernel(q_ref, k_ref, v_ref, m_ref, o_ref):
    t = ATT_T
    dh = B_HEAD_DIM
    qi = pl.program_id(2)
    neg_tri = m_ref[...]

    def sweep(kbs, diagonal, carry):
        heads = range(ATT_HEADS)
        starts = [pl.multiple_of(kb * t, t) for kb in kbs]
        chains = [(i, h) for i in range(len(kbs)) for h in heads]
        if any(diagonal):
            past = (lax.broadcasted_iota(jnp.int32, (t, t), 1)
                    < lax.broadcasted_iota(jnp.int32, (t, t), 0))
        z, suffix = {}, {}
        cs = [carry[h][0] for h in heads]
        accs = [carry[h][1] for h in heads]

        def scores(i, h):
            z[i, h] = lax.dot_general(q_ref[:, h * dh:(h + 1) * dh],
                                      k_ref[pl.ds(starts[i], t), h * dh:(h + 1) * dh],
                                      (((1,), (1,)), ((), ())), preferred_element_type=jnp.float32)

        def suffix_sum(i, h):
            sp = _softplus2(z[i, h])
            if diagonal[i]:
                sp = jnp.where(past, sp, 0.0)
            suffix[i, h] = jnp.dot(sp.astype(jnp.bfloat16), neg_tri,
                                   preferred_element_type=jnp.float32)

        def values(i, h):
            w = jnp.exp2(z[i, h] + suffix[i, h] + cs[h])
            if diagonal[i]:
                w = jnp.where(past, w, 0.0)
            accs[h] = accs[h] + jnp.dot(w.astype(jnp.bfloat16),
                                        v_ref[pl.ds(starts[i], t), h * dh:(h + 1) * dh],
                                        preferred_element_type=jnp.float32)
            cs[h] = cs[h] + suffix[i, h][:, 0:1]

        n = len(chains)
        for step in range(n + 2 * ATT_LAG):
            if 0 <= step - ATT_LAG < n:
                suffix_sum(*chains[step - ATT_LAG])
            if step < n:
                scores(*chains[step])
            if 0 <= step - 2 * ATT_LAG < n:
                values(*chains[step - 2 * ATT_LAG])
        return tuple((cs[h], accs[h]) for h in heads)

    zero = (jnp.zeros((t, 1), jnp.float32), jnp.zeros((t, dh), jnp.float32))
    carry = (zero,) * ATT_HEADS
    paired = qi % 2
    carry = lax.cond(paired == 1,
                     lambda c: sweep([qi, qi - 1], [True, False], c),
                     lambda c: sweep([qi], [True], c), carry)
    top = qi - paired
    carry = lax.fori_loop(
        0, top // 2,
        lambda p, c: sweep([top - 1 - 2 * p, top - 2 - 2 * p], [False, False], c), carry)
    for h in range(ATT_HEADS):
        o_ref[:, h * dh:(h + 1) * dh] = carry[h][1].astype(o_ref.dtype)


def _attention(proj, neg_tri, bsz, seq):
    t = ATT_T
    nq = seq // t
    hw = ATT_HEADS * B_HEAD_DIM
    nh = B_WIDTH // hw
    return pl.pallas_call(
        _attn_kernel,
        grid=(bsz, nh, nq),
        in_specs=[
            pl.BlockSpec((t, hw), lambda b, h, i: (b * nq + i, 2 * nh + h)),
            pl.BlockSpec((seq, hw), lambda b, h, i: (b, h)),
            pl.BlockSpec((seq, hw), lambda b, h, i: (b, nh + h)),
            pl.BlockSpec((t, t), lambda b, h, i: (0, 0)),
        ],
        out_specs=pl.BlockSpec((t, hw), lambda b, h, i: (b * nq + i, h)),
        out_shape=jax.ShapeDtypeStruct((bsz * seq, B_WIDTH), jnp.bfloat16),
        compiler_params=_params("parallel", "parallel", "arbitrary"),
        name="stick_breaking_attention",
    )(proj, proj, proj, neg_tri)


def _attn_out_kernel(o_ref, g_ref, x_ref, wo_ref, lg_ref, lb_ref, out_ref):
    s = (o_ref[...].astype(jnp.float32) * g_ref[...].astype(jnp.float32)).astype(jnp.bfloat16)
    y = jnp.dot(s, wo_ref[...], preferred_element_type=jnp.float32)
    out_ref[...] = _layer_norm(DN_ALPHA * x_ref[...] + y, lg_ref[...], lb_ref[...])


def _attn_out(o, proj, x, w_out, ln_g, ln_b):
    n, d = x.shape
    tm = MIX_TM
    return pl.pallas_call(
        _attn_out_kernel,
        grid=(n // tm,),
        in_specs=[
            pl.BlockSpec((tm, B_WIDTH), lambda i: (i, 0)),
            pl.BlockSpec((tm, B_WIDTH), lambda i: (i, 3)),
            pl.BlockSpec((tm, d), lambda i: (i, 0)),
            _resident((B_WIDTH, d)),
            _resident((1, d)),
            _resident((1, d)),
        ],
        out_specs=pl.BlockSpec((tm, d), lambda i: (i, 0)),
        out_shape=jax.ShapeDtypeStruct((n, d), jnp.float32),
        compiler_params=_params("parallel"),
        name="attn_out",
    )(o, proj, x, w_out, ln_g, ln_b)


def kernel(x, a_w_in, a_b_in, a_vln_g, a_vln_b, a_w_s, a_b_s, a_w_out, kv_w, b_w_in, b_w_out, ln_g, ln_b):
    bsz, seq, d = x.shape
    n = bsz * seq
    bf16 = jnp.bfloat16
    n_a = a_w_in.shape[0]
    n_b = b_w_in.shape[0]

    assert n_a >= 1, "the attention layers read the bf16 stream written by the last gMLP layer"
    xf = x.reshape(n, d)
    xb = None
    for i in range(n_a):
        hact = _gmlp_in(xf, a_w_in[i], a_b_in[i][None, :])
        xf, xb = _gmlp_out(hact, xf, a_vln_g[i][None, :], a_vln_b[i][None, :], a_w_s[i],
                           a_b_s[i][:, :, None], a_w_out[i].astype(bf16),
                           ln_g[i][None, :], ln_b[i][None, :])

    idx = jnp.arange(ATT_T)
    neg_tri = jnp.where(idx[:, None] >= idx[None, :], -1.0, 0.0).astype(bf16)
    assert n_b == 1, "the fused k|v|q|gate projection supports exactly one attention layer"
    proj = _attn_in(xb, kv_w, b_w_in[0])
    o = _attention(proj, neg_tri, bsz, seq)
    xf = _attn_out(o, proj, xf, b_w_out[0].astype(bf16), ln_g[n_a][None, :], ln_b[n_a][None, :])
    return xf.reshape(bsz, seq, d)
```

```python
import functools

import jax
import jax.numpy as jnp
from jax import lax
from jax.experimental import pallas as pl
from jax.experimental.pallas import tpu as pltpu

D_MODEL = 2048
DEPTH = 2
CHUNK = 128
A_WIDTH = 2 * D_MODEL
A_GROUPS = 8
A_GROUP_DIM = A_WIDTH // A_GROUPS
B_HEADS = 16
B_HEAD_DIM = D_MODEL // B_HEADS
B_WIDTH = B_HEADS * B_HEAD_DIM
DN_ALPHA = (2.0 * DEPTH) ** 0.25
LN_EPS = 1e-5

VMEM_LIMIT_BYTES = 56 * 1024 * 1024

PROJ_TM = 1024
PROJ_TN = 1024
PROJ_SUB = 256
PROJ_SLOTS = 3
MIX_TM = 256
ATT_T = 256
ATT_HEADS = 8
ATT_LAG = 2

_GELU_C = 0.7978845608028654
_LOG2_E = 1.4426950408889634


def _gelu_tanh(x):
    inner = x * (_GELU_C + (_GELU_C * 0.044715) * (x * x))
    return (0.5 * x) * (1.0 + jnp.tanh(inner))


def _silu(x):
    half = 0.5 * x
    return half * (1.0 + jnp.tanh(half))


def _layer_norm(z, g, b):
    mu = jnp.mean(z, axis=-1, keepdims=True)
    zc = z - mu
    var = jnp.mean(zc * zc, axis=-1, keepdims=True)
    return zc * lax.rsqrt(var + LN_EPS) * g + b


def _params(*semantics):
    return pltpu.CompilerParams(dimension_semantics=semantics, vmem_limit_bytes=VMEM_LIMIT_BYTES)


def _project_columns(x_ref, w_ref, o_ref, acc_ref, epilogue, b_ref=None):
    groups = w_ref.shape[1] // PROJ_SUB

    def product(n):
        cols = slice(n * PROJ_SUB, (n + 1) * PROJ_SUB)
        acc_ref[n % PROJ_SLOTS] = jnp.dot(x_ref[...], w_ref[:, cols].astype(jnp.bfloat16),
                                          preferred_element_type=jnp.float32)

    def finish(m):
        cols = slice(m * PROJ_SUB, (m + 1) * PROJ_SUB)
        h = acc_ref[m % PROJ_SLOTS]
        if b_ref is not None:
            h = h + b_ref[:, cols]
        o_ref[:, cols] = epilogue(h).astype(o_ref.dtype)

    lag = PROJ_SLOTS - 1
    for n in range(groups + lag):
        if n < groups:
            product(n)
        if n >= lag:
            finish(n - lag)


def _proj_scratch():
    return pltpu.VMEM((PROJ_SLOTS, PROJ_TM, PROJ_SUB), jnp.float32)


def _gmlp_in_kernel(x_ref, w_ref, b_ref, o_ref, xb_ref, acc_ref, *, silu_from):
    j = pl.program_id(1)

    @pl.when(j == 0)
    def _():
        xb_ref[...] = x_ref[...].astype(jnp.bfloat16)

    @pl.when(j < silu_from)
    def _():
        _project_columns(xb_ref, w_ref, o_ref, acc_ref, _gelu_tanh, b_ref)

    @pl.when(j >= silu_from)
    def _():
        _project_columns(xb_ref, w_ref, o_ref, acc_ref, _silu, b_ref)


def _gmlp_in(x, w, b):
    n, d = x.shape
    width = w.shape[1]
    grid = (n // PROJ_TM, width // PROJ_TN)
    return pl.pallas_call(
        functools.partial(_gmlp_in_kernel, silu_from=2 * A_WIDTH // PROJ_TN),
        grid=grid,
        in_specs=[
            pl.BlockSpec((PROJ_TM, d), lambda i, j: (i, 0)),
            pl.BlockSpec((d, PROJ_TN), lambda i, j: (0, j)),
            pl.BlockSpec((1, PROJ_TN), lambda i, j: (0, j)),
        ],
        out_specs=pl.BlockSpec((PROJ_TM, PROJ_TN), lambda i, j: (i, j)),
        out_shape=jax.ShapeDtypeStruct((n, width), jnp.bfloat16),
        scratch_shapes=[pltpu.VMEM((PROJ_TM, d), jnp.bfloat16), _proj_scratch()],
        compiler_params=_params("parallel", "arbitrary"),
        name="gmlp_in",
    )(x, w, b)


def _gmlp_out_kernel(u_ref, v_ref, g_ref, x_ref, vg_ref, vb_ref, ws_ref, bs_ref, wo_ref,
                     lg_ref, lb_ref, o_ref, ob_ref, s_ref):
    tm = u_ref.shape[0]
    vn = _layer_norm(v_ref[...].astype(jnp.float32), vg_ref[...], vb_ref[...]).astype(jnp.bfloat16)
    row = lax.broadcasted_iota(jnp.int32, (CHUNK, CHUNK), 0)
    col = lax.broadcasted_iota(jnp.int32, (CHUNK, CHUNK), 1)
    causal = col <= row
    for g in range(A_GROUPS):
        wt = jnp.where(causal, ws_ref[g], 0.0).astype(jnp.bfloat16)
        bias = bs_ref[g]
        cols = slice(g * A_GROUP_DIM, (g + 1) * A_GROUP_DIM)
        for c in range(tm // CHUNK):
            rows = slice(c * CHUNK, (c + 1) * CHUNK)
            mixed = jnp.dot(wt, vn[rows, cols], preferred_element_type=jnp.float32) + bias
            gate = u_ref[rows, cols].astype(jnp.float32) * g_ref[rows, cols].astype(jnp.float32)
            s_ref[rows, cols] = (gate * mixed).astype(jnp.bfloat16)
    y = jnp.dot(s_ref[...], wo_ref[...], preferred_element_type=jnp.float32)
    out = _layer_norm(DN_ALPHA * x_ref[...] + y, lg_ref[...], lb_ref[...])
    o_ref[...] = out
    ob_ref[...] = out.astype(jnp.bfloat16)


def _resident(shape):
    zeros = (0,) * len(shape)
    return pl.BlockSpec(shape, lambda i: zeros, pipeline_mode=pl.Buffered(1))


def _gmlp_out(hact, x, vln_g, vln_b, w_s, b_s, w_out, ln_g, ln_b):
    n, d = x.shape
    tm = MIX_TM
    return pl.pallas_call(
        _gmlp_out_kernel,
        grid=(n // tm,),
        in_specs=[
            pl.BlockSpec((tm, A_WIDTH), lambda i: (i, 0)),
            pl.BlockSpec((tm, A_WIDTH), lambda i: (i, 1)),
            pl.BlockSpec((tm, A_WIDTH), lambda i: (i, 2)),
            pl.BlockSpec((tm, d), lambda i: (i, 0)),
            _resident((1, A_WIDTH)),
            _resident((1, A_WIDTH)),
            _resident((A_GROUPS, CHUNK, CHUNK)),
            _resident((A_GROUPS, CHUNK, 1)),
            _resident((A_WIDTH, d)),
            _resident((1, d)),
            _resident((1, d)),
        ],
        out_specs=[
            pl.BlockSpec((tm, d), lambda i: (i, 0)),
            pl.BlockSpec((tm, d), lambda i: (i, 0)),
        ],
        out_shape=[
            jax.ShapeDtypeStruct((n, d), jnp.float32),
            jax.ShapeDtypeStruct((n, d), jnp.bfloat16),
        ],
        scratch_shapes=[pltpu.VMEM((tm, A_WIDTH), jnp.bfloat16)],
        compiler_params=_params("parallel"),
        name="gmlp_out",
    )(hact, hact, hact, x, vln_g, vln_b, w_s, b_s, w_out, ln_g, ln_b)


def _attn_in_kernel(x_ref, wkv_ref, wqg_ref, o_ref, wb_ref, acc_ref, *, q_from, gate_from, q_scale):
    j = pl.program_id(0)
    first_tile = pl.program_id(1) == 0

    @pl.when(jnp.logical_and(first_tile, j < q_from))
    def _():
        wb_ref[...] = wkv_ref[...].astype(jnp.bfloat16)

    @pl.when(jnp.logical_and(first_tile, j >= q_from))
    def _():
        wb_ref[...] = wqg_ref[...].astype(jnp.bfloat16)

    @pl.when(j < q_from)
    def _():
        _project_columns(x_ref, wb_ref, o_ref, acc_ref, lambda h: h)

    @pl.when(jnp.logical_and(j >= q_from, j < gate_from))
    def _():
        _project_columns(x_ref, wb_ref, o_ref, acc_ref, lambda h: h * q_scale)

    @pl.when(j >= gate_from)
    def _():
        _project_columns(x_ref, wb_ref, o_ref, acc_ref, _silu)


def _attn_in(xb, w_kv, w_qg):
    n, d = xb.shape
    kv_tiles = w_kv.shape[1] // PROJ_TN
    qg_tiles = w_qg.shape[1] // PROJ_TN
    kernel = functools.partial(
        _attn_in_kernel,
        q_from=kv_tiles,
        gate_from=kv_tiles + qg_tiles // 2,
        q_scale=B_HEAD_DIM ** -0.5 * _LOG2_E,
    )
    return pl.pallas_call(
        kernel,
        grid=(kv_tiles + qg_tiles, n // PROJ_TM),
        in_specs=[
            pl.BlockSpec((PROJ_TM, d), lambda j, i: (i, 0)),
            pl.BlockSpec((d, PROJ_TN), lambda j, i: (0, jnp.minimum(j, kv_tiles - 1))),
            pl.BlockSpec((d, PROJ_TN), lambda j, i: (0, jnp.maximum(j - kv_tiles, 0))),
        ],
        out_specs=pl.BlockSpec((PROJ_TM, PROJ_TN), lambda j, i: (i, j)),
        out_shape=jax.ShapeDtypeStruct((n, w_kv.shape[1] + w_qg.shape[1]), jnp.bfloat16),
        scratch_shapes=[pltpu.VMEM((d, PROJ_TN), jnp.bfloat16), _proj_scratch()],
        compiler_params=_params("parallel", "arbitrary"),
        name="attn_in",
    )(xb, w_kv, w_qg)


_EXP2_CLAMP = 64.0


def _softplus2(z):
    return jnp.maximum(z, jnp.log2(1.0 + jnp.exp2(jnp.minimum(z, _EXP2_CLAMP))))


def _attn_kernel(q_ref, k_ref, v_ref, m_ref, o_ref):
    t = ATT_T
    dh = B_HEAD_DIM
    qi = pl.program_id(2)
    neg_tri = m_ref[...]

    def sweep(kbs, diagonal, carry):
        heads = range(ATT_HEADS)
        starts = [pl.multiple_of(kb * t, t) for kb in kbs]
        chains = [(i, h) for i in range(len(kbs)) for h in heads]
        if any(diagonal):
            past = (lax.broadcasted_iota(jnp.int32, (t, t), 1)
                    < lax.broadcasted_iota(jnp.int32, (t, t), 0))
        z, suffix = {}, {}
        cs = [carry[h][0] for h in heads]
        accs = [carry[h][1] for h in heads]

        def scores(i, h):
            z[i, h] = lax.dot_general(q_ref[:, h * dh:(h + 1) * dh],
                                      k_ref[pl.ds(starts[i], t), h * dh:(h + 1) * dh],
                                      (((1,), (1,)), ((), ())), preferred_element_type=jnp.float32)

        def suffix_sum(i, h):
            sp = _softplus2(z[i, h])
            if diagonal[i]:
                sp = jnp.where(past, sp, 0.0)
            suffix[i, h] = jnp.dot(sp.astype(jnp.bfloat16), neg_tri,
                                   preferred_element_type=jnp.float32)

        def values(i, h):
            w = jnp.exp2(z[i, h] + suffix[i, h] + cs[h])
            if diagonal[i]:
                w = jnp.where(past, w, 0.0)
            accs[h] = accs[h] + jnp.dot(w.astype(jnp.bfloat16),
                                        v_ref[pl.ds(starts[i], t), h * dh:(h + 1) * dh],
                                        preferred_element_type=jnp.float32)
            cs[h] = cs[h] + suffix[i, h][:, 0:1]

        n = len(chains)
        for step in range(n + 2 * ATT_LAG):
            if 0 <= step - ATT_LAG < n:
                suffix_sum(*chains[step - ATT_LAG])
            if step < n:
                scores(*chains[step])
            if 0 <= step - 2 * ATT_LAG < n:
                values(*chains[step - 2 * ATT_LAG])
        return tuple((cs[h], accs[h]) for h in heads)

    zero = (jnp.zeros((t, 1), jnp.float32), jnp.zeros((t, dh), jnp.float32))
    carry = (zero,) * ATT_HEADS
    paired = qi % 2
    carry = lax.cond(paired == 1,
                     lambda c: sweep([qi, qi - 1], [True, False], c),
                     lambda c: sweep([qi], [True], c), carry)
    top = qi - paired
    carry = lax.fori_loop(
        0, top // 2,
        lambda p, c: sweep([top - 1 - 2 * p, top - 2 - 2 * p], [False, False], c), carry)
    for h in range(ATT_HEADS):
        o_ref[:, h * dh:(h + 1) * dh] = carry[h][1].astype(o_ref.dtype)


def _attention(proj, neg_tri, bsz, seq):
    t = ATT_T
    nq = seq // t
    hw = ATT_HEADS * B_HEAD_DIM
    nh = B_WIDTH // hw
    return pl.pallas_call(
        _attn_kernel,
        grid=(bsz, nh, nq),
        in_specs=[
            pl.BlockSpec((t, hw), lambda b, h, i: (b * nq + i, 2 * nh + h)),
            pl.BlockSpec((seq, hw), lambda b, h, i: (b, h)),
            pl.BlockSpec((seq, hw), lambda b, h, i: (b, nh + h)),
            pl.BlockSpec((t, t), lambda b, h, i: (0, 0)),
        ],
        out_specs=pl.BlockSpec((t, hw), lambda b, h, i: (b * nq + i, h)),
        out_shape=jax.ShapeDtypeStruct((bsz * seq, B_WIDTH), jnp.bfloat16),
        compiler_params=_params("parallel", "parallel", "arbitrary"),
        name="stick_breaking_attention",
    )(proj, proj, proj, neg_tri)


def _attn_out_kernel(o_ref, g_ref, x_ref, wo_ref, lg_ref, lb_ref, out_ref):
    s = (o_ref[...].astype(jnp.float32) * g_ref[...].astype(jnp.float32)).astype(jnp.bfloat16)
    y = jnp.dot(s, wo_ref[...], preferred_element_type=jnp.float32)
    out_ref[...] = _layer_norm(DN_ALPHA * x_ref[...] + y, lg_ref[...], lb_ref[...])


def _attn_out(o, proj, x, w_out, ln_g, ln_b):
    n, d = x.shape
    tm = MIX_TM
    return pl.pallas_call(
        _attn_out_kernel,
        grid=(n // tm,),
        in_specs=[
            pl.BlockSpec((tm, B_WIDTH), lambda i: (i, 0)),
            pl.BlockSpec((tm, B_WIDTH), lambda i: (i, 3)),
            pl.BlockSpec((tm, d), lambda i: (i, 0)),
            _resident((B_WIDTH, d)),
            _resident((1, d)),
            _resident((1, d)),
        ],
        out_specs=pl.BlockSpec((tm, d), lambda i: (i, 0)),
        out_shape=jax.ShapeDtypeStruct((n, d), jnp.float32),
        compiler_params=_params("parallel"),
        name="attn_out",
    )(o, proj, x, w_out, ln_g, ln_b)


def kernel(x, a_w_in, a_b_in, a_vln_g, a_vln_b, a_w_s, a_b_s, a_w_out, kv_w, b_w_in, b_w_out, ln_g, ln_b):
    bsz, seq, d = x.shape
    n = bsz * seq
    bf16 = jnp.bfloat16
    n_a = a_w_in.shape[0]
    n_b = b_w_in.shape[0]

    assert n_a >= 1, "the attention layers read the bf16 stream written by the last gMLP layer"
    xf = x.reshape(n, d)
    xb = None
    for i in range(n_a):
        hact = _gmlp_in(xf, a_w_in[i], a_b_in[i][None, :])
        xf, xb = _gmlp_out(hact, xf, a_vln_g[i][None, :], a_vln_b[i][None, :], a_w_s[i],
                           a_b_s[i][:, :, None], a_w_out[i].astype(bf16),
                           ln_g[i][None, :], ln_b[i][None, :])

    idx = jnp.arange(ATT_T)
    neg_tri = jnp.where(idx[:, None] >= idx[None, :], -1.0, 0.0).astype(bf16)
    assert n_b == 1, "the fused k|v|q|gate projection supports exactly one attention layer"
    proj = _attn_in(xb, kv_w, b_w_in[0])
    o = _attention(proj, neg_tri, bsz, seq)
    xf = _attn_out(o, proj, xf, b_w_out[0].astype(bf16), ln_g[n_a][None, :], ln_b[n_a][None, :])
    return xf.reshape(bsz, seq, d)
```

```python
import functools

import jax
import jax.numpy as jnp
from jax import lax
from jax.experimental import pallas as pl
from jax.experimental.pallas import tpu as pltpu

D_MODEL = 2048
DEPTH = 2
CHUNK = 128
A_WIDTH = 2 * D_MODEL
A_GROUPS = 8
A_GROUP_DIM = A_WIDTH // A_GROUPS
B_HEADS = 16
B_HEAD_DIM = D_MODEL // B_HEADS
B_WIDTH = B_HEADS * B_HEAD_DIM
DN_ALPHA = (2.0 * DEPTH) ** 0.25
LN_EPS = 1e-5

VMEM_LIMIT_BYTES = 56 * 1024 * 1024

PROJ_TM = 1024
PROJ_TN = 1024
PROJ_SUB = 256
PROJ_SLOTS = 3
WNEXT_BLOCKS = 32
MIX_TM = 256
ATTN_OUT_TM = 512
ATT_T = 256
ATT_HEADS = 8
ATT_LAG = 2

_GELU_C = 0.7978845608028654
_LOG2_E = 1.4426950408889634


def _gelu_tanh(x):
    inner = x * (_GELU_C + (_GELU_C * 0.044715) * (x * x))
    return (0.5 * x) * (1.0 + jnp.tanh(inner))


def _silu(x):
    half = 0.5 * x
    return half * (1.0 + jnp.tanh(half))


def _layer_norm(z, g, b):
    mu = jnp.mean(z, axis=-1, keepdims=True)
    zc = z - mu
    var = jnp.mean(zc * zc, axis=-1, keepdims=True)
    return zc * lax.rsqrt(var + LN_EPS) * g + b


def _params(*semantics):
    return pltpu.CompilerParams(dimension_semantics=semantics, vmem_limit_bytes=VMEM_LIMIT_BYTES)


def _project_columns(x_ref, w_ref, o_ref, acc_ref, epilogue, b_ref=None):
    groups = w_ref.shape[1] // PROJ_SUB

    def product(n):
        cols = slice(n * PROJ_SUB, (n + 1) * PROJ_SUB)
        acc_ref[n % PROJ_SLOTS] = jnp.dot(x_ref[...], w_ref[:, cols].astype(jnp.bfloat16),
                                          preferred_element_type=jnp.float32)

    def finish(m):
        cols = slice(m * PROJ_SUB, (m + 1) * PROJ_SUB)
        h = acc_ref[m % PROJ_SLOTS]
        if b_ref is not None:
            h = h + b_ref[:, cols]
        o_ref[:, cols] = epilogue(h).astype(o_ref.dtype)

    lag = PROJ_SLOTS - 1
    for n in range(groups + lag):
        if n < groups:
            product(n)
        if n >= lag:
            finish(n - lag)


def _proj_scratch():
    return pltpu.VMEM((PROJ_SLOTS, PROJ_TM, PROJ_SUB), jnp.float32)


def _round_rows(step, src_ref, dst_ref, blocks):
    @pl.when(step < blocks)
    def _():
        dst_ref[...] = src_ref[...].astype(jnp.bfloat16)


def _round_rows_specs(w, blocks, step_of):
    rows = w.shape[0] // blocks
    index = lambda *ids: (jnp.minimum(step_of(*ids), blocks - 1), 0)
    spec = pl.BlockSpec((rows, w.shape[1]), index)
    return spec, spec, jax.ShapeDtypeStruct(w.shape, jnp.bfloat16)


def _gmlp_in_kernel(x_ref, w_ref, b_ref, wnext_ref, o_ref, wnext_bf_ref, xb_ref, acc_ref, *,
                    silu_from, wnext_blocks):
    j = pl.program_id(1)
    _round_rows(pl.program_id(0) * pl.num_programs(1) + j, wnext_ref, wnext_bf_ref, wnext_blocks)

    @pl.when(j == 0)
    def _():
        xb_ref[...] = x_ref[...].astype(jnp.bfloat16)

    @pl.when(j < silu_from)
    def _():
        _project_columns(xb_ref, w_ref, o_ref, acc_ref, _gelu_tanh, b_ref)

    @pl.when(j >= silu_from)
    def _():
        _project_columns(xb_ref, w_ref, o_ref, acc_ref, _silu, b_ref)


def _gmlp_in(x, w, b, w_next):
    n, d = x.shape
    width = w.shape[1]
    grid = (n // PROJ_TM, width // PROJ_TN)
    wnext_in, wnext_out, wnext_shape = _round_rows_specs(
        w_next, WNEXT_BLOCKS, lambda i, j: i * grid[1] + j)
    return pl.pallas_call(
        functools.partial(_gmlp_in_kernel, silu_from=2 * A_WIDTH // PROJ_TN,
                          wnext_blocks=WNEXT_BLOCKS),
        grid=grid,
        in_specs=[
            pl.BlockSpec((PROJ_TM, d), lambda i, j: (i, 0)),
            pl.BlockSpec((d, PROJ_TN), lambda i, j: (0, j)),
            pl.BlockSpec((1, PROJ_TN), lambda i, j: (0, j)),
            wnext_in,
        ],
        out_specs=[pl.BlockSpec((PROJ_TM, PROJ_TN), lambda i, j: (i, j)), wnext_out],
        out_shape=[jax.ShapeDtypeStruct((n, width), jnp.bfloat16), wnext_shape],
        scratch_shapes=[pltpu.VMEM((PROJ_TM, d), jnp.bfloat16), _proj_scratch()],
        compiler_params=_params("arbitrary", "arbitrary"),
        name="gmlp_in",
    )(x, w, b, w_next)


def _gmlp_out_kernel(u_ref, v_ref, g_ref, x_ref, vg_ref, vb_ref, ws_ref, bs_ref, wo_ref,
                     lg_ref, lb_ref, o_ref, ob_ref, s_ref):
    tm = u_ref.shape[0]
    vn = _layer_norm(v_ref[...].astype(jnp.float32), vg_ref[...], vb_ref[...]).astype(jnp.bfloat16)
    row = lax.broadcasted_iota(jnp.int32, (CHUNK, CHUNK), 0)
    col = lax.broadcasted_iota(jnp.int32, (CHUNK, CHUNK), 1)
    causal = col <= row
    for g in range(A_GROUPS):
        wt = jnp.where(causal, ws_ref[g], 0.0).astype(jnp.bfloat16)
        bias = bs_ref[g]
        cols = slice(g * A_GROUP_DIM, (g + 1) * A_GROUP_DIM)
        for c in range(tm // CHUNK):
            rows = slice(c * CHUNK, (c + 1) * CHUNK)
            mixed = jnp.dot(wt, vn[rows, cols], preferred_element_type=jnp.float32) + bias
            gate = u_ref[rows, cols].astype(jnp.float32) * g_ref[rows, cols].astype(jnp.float32)
            s_ref[rows, cols] = (gate * mixed).astype(jnp.bfloat16)
    y = jnp.dot(s_ref[...], wo_ref[...], preferred_element_type=jnp.float32)
    out = _layer_norm(DN_ALPHA * x_ref[...] + y, lg_ref[...], lb_ref[...])
    o_ref[...] = out
    ob_ref[...] = out.astype(jnp.bfloat16)


def _resident(shape):
    zeros = (0,) * len(shape)
    return pl.BlockSpec(shape, lambda i: zeros, pipeline_mode=pl.Buffered(1))


def _gmlp_out(hact, x, vln_g, vln_b, w_s, b_s, w_out, ln_g, ln_b):
    n, d = x.shape
    tm = MIX_TM
    return pl.pallas_call(
        _gmlp_out_kernel,
        grid=(n // tm,),
        in_specs=[
            pl.BlockSpec((tm, A_WIDTH), lambda i: (i, 0)),
            pl.BlockSpec((tm, A_WIDTH), lambda i: (i, 1)),
            pl.BlockSpec((tm, A_WIDTH), lambda i: (i, 2)),
            pl.BlockSpec((tm, d), lambda i: (i, 0)),
            _resident((1, A_WIDTH)),
            _resident((1, A_WIDTH)),
            _resident((A_GROUPS, CHUNK, CHUNK)),
            _resident((A_GROUPS, CHUNK, 1)),
            _resident((A_WIDTH, d)),
            _resident((1, d)),
            _resident((1, d)),
        ],
        out_specs=[
            pl.BlockSpec((tm, d), lambda i: (i, 0)),
            pl.BlockSpec((tm, d), lambda i: (i, 0)),
        ],
        out_shape=[
            jax.ShapeDtypeStruct((n, d), jnp.float32),
            jax.ShapeDtypeStruct((n, d), jnp.bfloat16),
        ],
        scratch_shapes=[pltpu.VMEM((tm, A_WIDTH), jnp.bfloat16)],
        compiler_params=_params("parallel"),
        name="gmlp_out",
    )(hact, hact, hact, x, vln_g, vln_b, w_s, b_s, w_out, ln_g, ln_b)


def _attn_in_kernel(x_ref, wkv_ref, wqg_ref, wnext_ref, o_ref, wnext_bf_ref, wb_ref, acc_ref, *,
                    q_from, gate_from, q_scale, wnext_blocks):
    j = pl.program_id(0)
    first_tile = pl.program_id(1) == 0
    _round_rows(j * pl.num_programs(1) + pl.program_id(1), wnext_ref, wnext_bf_ref, wnext_blocks)

    @pl.when(jnp.logical_and(first_tile, j < q_from))
    def _():
        wb_ref[...] = wkv_ref[...].astype(jnp.bfloat16)

    @pl.when(jnp.logical_and(first_tile, j >= q_from))
    def _():
        wb_ref[...] = wqg_ref[...].astype(jnp.bfloat16)

    @pl.when(j < q_from)
    def _():
        _project_columns(x_ref, wb_ref, o_ref, acc_ref, lambda h: h)

    @pl.when(jnp.logical_and(j >= q_from, j < gate_from))
    def _():
        _project_columns(x_ref, wb_ref, o_ref, acc_ref, lambda h: h * q_scale)

    @pl.when(j >= gate_from)
    def _():
        _project_columns(x_ref, wb_ref, o_ref, acc_ref, _silu)


def _attn_in(xb, w_kv, w_qg, w_next):
    n, d = xb.shape
    kv_tiles = w_kv.shape[1] // PROJ_TN
    qg_tiles = w_qg.shape[1] // PROJ_TN
    grid = (kv_tiles + qg_tiles, n // PROJ_TM)
    wnext_in, wnext_out, wnext_shape = _round_rows_specs(
        w_next, WNEXT_BLOCKS, lambda j, i: j * grid[1] + i)
    kernel = functools.partial(
        _attn_in_kernel,
        q_from=kv_tiles,
        gate_from=kv_tiles + qg_tiles // 2,
        q_scale=B_HEAD_DIM ** -0.5 * _LOG2_E,
        wnext_blocks=WNEXT_BLOCKS,
    )
    return pl.pallas_call(
        kernel,
        grid=grid,
        in_specs=[
            pl.BlockSpec((PROJ_TM, d), lambda j, i: (i, 0)),
            pl.BlockSpec((d, PROJ_TN), lambda j, i: (0, jnp.minimum(j, kv_tiles - 1))),
            pl.BlockSpec((d, PROJ_TN), lambda j, i: (0, jnp.maximum(j - kv_tiles, 0))),
            wnext_in,
        ],
        out_specs=[pl.BlockSpec((PROJ_TM, PROJ_TN), lambda j, i: (i, j)), wnext_out],
        out_shape=[jax.ShapeDtypeStruct((n, w_kv.shape[1] + w_qg.shape[1]), jnp.bfloat16), wnext_shape],
        scratch_shapes=[pltpu.VMEM((d, PROJ_TN), jnp.bfloat16), _proj_scratch()],
        compiler_params=_params("arbitrary", "arbitrary"),
        name="attn_in",
    )(xb, w_kv, w_qg, w_next)


_EXP2_CLAMP = 64.0


def _softplus2(z):
    return jnp.maximum(z, jnp.log2(1.0 + jnp.exp2(jnp.minimum(z, _EXP2_CLAMP))))


def _attn_kernel(q_ref, k_ref, v_ref, m_ref, o_ref):
    t = ATT_T
    dh = B_HEAD_DIM
    qi = pl.program_id(2)
    neg_tri = m_ref[...]

    def sweep(kbs, diagonal, carry):
        heads = range(ATT_HEADS)
        starts = [pl.multiple_of(kb * t, t) for kb in kbs]
        chains = [(i, h) for i in range(len(kbs)) for h in heads]
        if any(diagonal):
            past = (lax.broadcasted_iota(jnp.int32, (t, t), 1)
                    < lax.broadcasted_iota(jnp.int32, (t, t), 0))
        z, suffix = {}, {}
        cs = [carry[h][0] for h in heads]
        accs = [carry[h][1] for h in heads]

        def scores(i, h):
            z[i, h] = lax.dot_general(q_ref[:, h * dh:(h + 1) * dh],
                                      k_ref[pl.ds(starts[i], t), h * dh:(h + 1) * dh],
                                      (((1,), (1,)), ((), ())), preferred_element_type=jnp.float32)

        def suffix_sum(i, h):
            sp = _softplus2(z[i, h])
            if diagonal[i]:
                sp = jnp.where(past, sp, 0.0)
            suffix[i, h] = jnp.dot(sp.astype(jnp.bfloat16), neg_tri,
                                   preferred_element_type=jnp.float32)

        def values(i, h):
            w = jnp.exp2(z[i, h] + suffix[i, h] + cs[h])
            if diagonal[i]:
                w = jnp.where(past, w, 0.0)
            accs[h] = accs[h] + jnp.dot(w.astype(jnp.bfloat16),
                                        v_ref[pl.ds(starts[i], t), h * dh:(h + 1) * dh],
                                        preferred_element_type=jnp.float32)
            cs[h] = cs[h] + suffix[i, h][:, 0:1]

        n = len(chains)
        for step in range(n + 2 * ATT_LAG):
            if 0 <= step - ATT_LAG < n:
                suffix_sum(*chains[step - ATT_LAG])
            if step < n:
                scores(*chains[step])
            if 0 <= step - 2 * ATT_LAG < n:
                values(*chains[step - 2 * ATT_LAG])
        return tuple((cs[h], accs[h]) for h in heads)

    zero = (jnp.zeros((t, 1), jnp.float32), jnp.zeros((t, dh), jnp.float32))
    carry = (zero,) * ATT_HEADS
    paired = qi % 2
    carry = lax.cond(paired == 1,
                     lambda c: sweep([qi, qi - 1], [True, False], c),
                     lambda c: sweep([qi], [True], c), carry)
    top = qi - paired
    carry = lax.fori_loop(
        0, top // 2,
        lambda p, c: sweep([top - 1 - 2 * p, top - 2 - 2 * p], [False, False], c), carry)
    for h in range(ATT_HEADS):
        o_ref[:, h * dh:(h + 1) * dh] = carry[h][1].astype(o_ref.dtype)


def _attention(proj, neg_tri, bsz, seq):
    t = ATT_T
    nq = seq // t
    hw = ATT_HEADS * B_HEAD_DIM
    nh = B_WIDTH // hw
    return pl.pallas_call(
        _attn_kernel,
        grid=(bsz, nh, nq),
        in_specs=[
            pl.BlockSpec((t, hw), lambda b, h, i: (b * nq + i, 2 * nh + h)),
            pl.BlockSpec((seq, hw), lambda b, h, i: (b, h)),
            pl.BlockSpec((seq, hw), lambda b, h, i: (b, nh + h)),
            pl.BlockSpec((t, t), lambda b, h, i: (0, 0)),
        ],
        out_specs=pl.BlockSpec((t, hw), lambda b, h, i: (b * nq + i, h)),
        out_shape=jax.ShapeDtypeStruct((bsz * seq, B_WIDTH), jnp.bfloat16),
        compiler_params=_params("parallel", "parallel", "arbitrary"),
        name="stick_breaking_attention",
    )(proj, proj, proj, neg_tri)


def _attn_out_kernel(o_ref, g_ref, x_ref, wo_ref, lg_ref, lb_ref, out_ref):
    s = (o_ref[...].astype(jnp.float32) * g_ref[...].astype(jnp.float32)).astype(jnp.bfloat16)
    y = jnp.dot(s, wo_ref[...], preferred_element_type=jnp.float32)
    out_ref[...] = _layer_norm(DN_ALPHA * x_ref[...] + y, lg_ref[...], lb_ref[...])


def _attn_out(o, proj, x, w_out, ln_g, ln_b):
    n, d = x.shape
    tm = ATTN_OUT_TM
    return pl.pallas_call(
        _attn_out_kernel,
        grid=(n // tm,),
        in_specs=[
            pl.BlockSpec((tm, B_WIDTH), lambda i: (i, 0)),
            pl.BlockSpec((tm, B_WIDTH), lambda i: (i, 3)),
            pl.BlockSpec((tm, d), lambda i: (i, 0)),
            _resident((B_WIDTH, d)),
            _resident((1, d)),
            _resident((1, d)),
        ],
        out_specs=pl.BlockSpec((tm, d), lambda i: (i, 0)),
        out_shape=jax.ShapeDtypeStruct((n, d), jnp.float32),
        compiler_params=_params("parallel"),
        name="attn_out",
    )(o, proj, x, w_out, ln_g, ln_b)


def kernel(x, a_w_in, a_b_in, a_vln_g, a_vln_b, a_w_s, a_b_s, a_w_out, kv_w, b_w_in, b_w_out, ln_g, ln_b):
    bsz, seq, d = x.shape
    n = bsz * seq
    bf16 = jnp.bfloat16
    n_a = a_w_in.shape[0]
    n_b = b_w_in.shape[0]

    assert n_a >= 1, "the attention layers read the bf16 stream written by the last gMLP layer"
    xf = x.reshape(n, d)
    xb = None
    for i in range(n_a):
        hact, w_out = _gmlp_in(xf, a_w_in[i], a_b_in[i][None, :], a_w_out[i])
        xf, xb = _gmlp_out(hact, xf, a_vln_g[i][None, :], a_vln_b[i][None, :], a_w_s[i],
                           a_b_s[i][:, :, None], w_out, ln_g[i][None, :], ln_b[i][None, :])

    idx = jnp.arange(ATT_T)
    neg_tri = jnp.where(idx[:, None] >= idx[None, :], -1.0, 0.0).astype(bf16)
    assert n_b == 1, "the fused k|v|q|gate projection supports exactly one attention layer"
    proj, w_out = _attn_in(xb, kv_w, b_w_in[0], b_w_out[0])
    o = _attention(proj, neg_tri, bsz, seq)
    xf = _attn_out(o, proj, xf, w_out, ln_g[n_a][None, :], ln_b[n_a][None, :])
    return xf.reshape(bsz, seq, d)
```

```python
import functools

import jax
import jax.numpy as jnp
from jax import lax
from jax.experimental import pallas as pl
from jax.experimental.pallas import tpu as pltpu

D_MODEL = 2048
DEPTH = 2
CHUNK = 128
A_WIDTH = 2 * D_MODEL
A_GROUPS = 8
A_GROUP_DIM = A_WIDTH // A_GROUPS
B_HEADS = 16
B_HEAD_DIM = D_MODEL // B_HEADS
B_WIDTH = B_HEADS * B_HEAD_DIM
DN_ALPHA = (2.0 * DEPTH) ** 0.25
LN_EPS = 1e-5

VMEM_LIMIT_BYTES = 56 * 1024 * 1024

PROJ_TM = 1024
PROJ_TN = 1024
PROJ_SUB = 256
PROJ_SLOTS = 3
WNEXT_BLOCKS = 32
MIX_TM = 256
ATTN_OUT_TM = 512
ATT_T = 256
ATT_HEADS = 8
ATT_LAG = 2

_GELU_C = 0.7978845608028654
_LOG2_E = 1.4426950408889634


def _gelu_tanh(x):
    inner = x * (_GELU_C + (_GELU_C * 0.044715) * (x * x))
    return (0.5 * x) * (1.0 + jnp.tanh(inner))


def _silu(x):
    half = 0.5 * x
    return half * (1.0 + jnp.tanh(half))


def _layer_norm(z, g, b):
    mu = jnp.mean(z, axis=-1, keepdims=True)
    zc = z - mu
    var = jnp.mean(zc * zc, axis=-1, keepdims=True)
    return zc * lax.rsqrt(var + LN_EPS) * g + b


def _params(*semantics):
    return pltpu.CompilerParams(dimension_semantics=semantics, vmem_limit_bytes=VMEM_LIMIT_BYTES)


def _project_columns(x_ref, w_ref, o_ref, acc_ref, epilogue, b_ref=None):
    groups = w_ref.shape[1] // PROJ_SUB

    def product(n):
        cols = slice(n * PROJ_SUB, (n + 1) * PROJ_SUB)
        acc_ref[n % PROJ_SLOTS] = jnp.dot(x_ref[...], w_ref[:, cols].astype(jnp.bfloat16),
                                          preferred_element_type=jnp.float32)

    def finish(m):
        cols = slice(m * PROJ_SUB, (m + 1) * PROJ_SUB)
        h = acc_ref[m % PROJ_SLOTS]
        if b_ref is not None:
            h = h + b_ref[:, cols]
        o_ref[:, cols] = epilogue(h).astype(o_ref.dtype)

    lag = PROJ_SLOTS - 1
    for n in range(groups + lag):
        if n < groups:
            product(n)
        if n >= lag:
            finish(n - lag)


def _proj_scratch():
    return pltpu.VMEM((PROJ_SLOTS, PROJ_TM, PROJ_SUB), jnp.float32)


def _round_rows(step, src_ref, dst_ref, blocks):
    @pl.when(step < blocks)
    def _():
        dst_ref[...] = src_ref[...].astype(jnp.bfloat16)


def _round_rows_specs(w, blocks, step_of):
    rows = w.shape[0] // blocks
    index = lambda *ids: (jnp.minimum(step_of(*ids), blocks - 1), 0)
    spec = pl.BlockSpec((rows, w.shape[1]), index)
    return spec, spec, jax.ShapeDtypeStruct(w.shape, jnp.bfloat16)


def _gmlp_in_kernel(x_ref, w_ref, b_ref, wnext_ref, o_ref, wnext_bf_ref, xb_ref, acc_ref, *,
                    silu_from, wnext_blocks):
    j = pl.program_id(1)
    _round_rows(pl.program_id(0) * pl.num_programs(1) + j, wnext_ref, wnext_bf_ref, wnext_blocks)

    @pl.when(j == 0)
    def _():
        xb_ref[...] = x_ref[...].astype(jnp.bfloat16)

    @pl.when(j < silu_from)
    def _():
        _project_columns(xb_ref, w_ref, o_ref, acc_ref, _gelu_tanh, b_ref)

    @pl.when(j >= silu_from)
    def _():
        _project_columns(xb_ref, w_ref, o_ref, acc_ref, _silu, b_ref)


def _gmlp_in(x, w, b, w_next):
    n, d = x.shape
    width = w.shape[1]
    grid = (n // PROJ_TM, width // PROJ_TN)
    wnext_in, wnext_out, wnext_shape = _round_rows_specs(
        w_next, WNEXT_BLOCKS, lambda i, j: i * grid[1] + j)
    return pl.pallas_call(
        functools.partial(_gmlp_in_kernel, silu_from=2 * A_WIDTH // PROJ_TN,
                          wnext_blocks=WNEXT_BLOCKS),
        grid=grid,
        in_specs=[
            pl.BlockSpec((PROJ_TM, d), lambda i, j: (i, 0)),
            pl.BlockSpec((d, PROJ_TN), lambda i, j: (0, j)),
            pl.BlockSpec((1, PROJ_TN), lambda i, j: (0, j)),
            wnext_in,
        ],
        out_specs=[pl.BlockSpec((PROJ_TM, PROJ_TN), lambda i, j: (i, j)), wnext_out],
        out_shape=[jax.ShapeDtypeStruct((n, width), jnp.bfloat16), wnext_shape],
        scratch_shapes=[pltpu.VMEM((PROJ_TM, d), jnp.bfloat16), _proj_scratch()],
        compiler_params=_params("arbitrary", "arbitrary"),
        name="gmlp_in",
    )(x, w, b, w_next)


def _gmlp_out_kernel(u_ref, v_ref, g_ref, x_ref, vg_ref, vb_ref, ws_ref, bs_ref, wo_ref,
                     lg_ref, lb_ref, o_ref, ob_ref, s_ref):
    tm = u_ref.shape[0]
    vn = _layer_norm(v_ref[...].astype(jnp.float32), vg_ref[...], vb_ref[...]).astype(jnp.bfloat16)
    row = lax.broadcasted_iota(jnp.int32, (CHUNK, CHUNK), 0)
    col = lax.broadcasted_iota(jnp.int32, (CHUNK, CHUNK), 1)
    causal = col <= row
    for g in range(A_GROUPS):
        wt = jnp.where(causal, ws_ref[g], 0.0).astype(jnp.bfloat16)
        bias = bs_ref[g]
        cols = slice(g * A_GROUP_DIM, (g + 1) * A_GROUP_DIM)
        for c in range(tm // CHUNK):
            rows = slice(c * CHUNK, (c + 1) * CHUNK)
            mixed = jnp.dot(wt, vn[rows, cols], preferred_element_type=jnp.float32) + bias
            gate = u_ref[rows, cols].astype(jnp.float32) * g_ref[rows, cols].astype(jnp.float32)
            s_ref[rows, cols] = (gate * mixed).astype(jnp.bfloat16)
    y = jnp.dot(s_ref[...], wo_ref[...], preferred_element_type=jnp.float32)
    out = _layer_norm(DN_ALPHA * x_ref[...] + y, lg_ref[...], lb_ref[...])
    o_ref[...] = out
    ob_ref[...] = out.astype(jnp.bfloat16)


def _resident(shape):
    zeros = (0,) * len(shape)
    return pl.BlockSpec(shape, lambda i: zeros, pipeline_mode=pl.Buffered(1))


def _gmlp_out(hact, x, vln_g, vln_b, w_s, b_s, w_out, ln_g, ln_b):
    n, d = x.shape
    tm = MIX_TM
    return pl.pallas_call(
        _gmlp_out_kernel,
        grid=(n // tm,),
        in_specs=[
            pl.BlockSpec((tm, A_WIDTH), lambda i: (i, 0)),
            pl.BlockSpec((tm, A_WIDTH), lambda i: (i, 1)),
            pl.BlockSpec((tm, A_WIDTH), lambda i: (i, 2)),
            pl.BlockSpec((tm, d), lambda i: (i, 0)),
            _resident((1, A_WIDTH)),
            _resident((1, A_WIDTH)),
            _resident((A_GROUPS, CHUNK, CHUNK)),
            _resident((A_GROUPS, CHUNK, 1)),
            _resident((A_WIDTH, d)),
            _resident((1, d)),
            _resident((1, d)),
        ],
        out_specs=[
            pl.BlockSpec((tm, d), lambda i: (i, 0)),
            pl.BlockSpec((tm, d), lambda i: (i, 0)),
        ],
        out_shape=[
            jax.ShapeDtypeStruct((n, d), jnp.float32),
            jax.ShapeDtypeStruct((n, d), jnp.bfloat16),
        ],
        scratch_shapes=[pltpu.VMEM((tm, A_WIDTH), jnp.bfloat16)],
        compiler_params=_params("parallel"),
        name="gmlp_out",
    )(hact, hact, hact, x, vln_g, vln_b, w_s, b_s, w_out, ln_g, ln_b)


def _attn_in_kernel(x_ref, wkv_ref, wqg_ref, wnext_ref, o_ref, wnext_bf_ref, wb_ref, acc_ref, *,
                    q_from, gate_from, q_scale, wnext_blocks):
    j = pl.program_id(0)
    first_tile = pl.program_id(1) == 0
    _round_rows(j * pl.num_programs(1) + pl.program_id(1), wnext_ref, wnext_bf_ref, wnext_blocks)

    @pl.when(jnp.logical_and(first_tile, j < q_from))
    def _():
        wb_ref[...] = wkv_ref[...].astype(jnp.bfloat16)

    @pl.when(jnp.logical_and(first_tile, j >= q_from))
    def _():
        wb_ref[...] = wqg_ref[...].astype(jnp.bfloat16)

    @pl.when(j < q_from)
    def _():
        _project_columns(x_ref, wb_ref, o_ref, acc_ref, lambda h: h)

    @pl.when(jnp.logical_and(j >= q_from, j < gate_from))
    def _():
        _project_columns(x_ref, wb_ref, o_ref, acc_ref, lambda h: h * q_scale)

    @pl.when(j >= gate_from)
    def _():
        _project_columns(x_ref, wb_ref, o_ref, acc_ref, _silu)


def _attn_in(xb, w_kv, w_qg, w_next):
    n, d = xb.shape
    kv_tiles = w_kv.shape[1] // PROJ_TN
    qg_tiles = w_qg.shape[1] // PROJ_TN
    grid = (kv_tiles + qg_tiles, n // PROJ_TM)
    wnext_in, wnext_out, wnext_shape = _round_rows_specs(
        w_next, WNEXT_BLOCKS, lambda j, i: j * grid[1] + i)
    kernel = functools.partial(
        _attn_in_kernel,
        q_from=kv_tiles,
        gate_from=kv_tiles + qg_tiles // 2,
        q_scale=B_HEAD_DIM ** -0.5 * _LOG2_E,
        wnext_blocks=WNEXT_BLOCKS,
    )
    return pl.pallas_call(
        kernel,
        grid=grid,
        in_specs=[
            pl.BlockSpec((PROJ_TM, d), lambda j, i: (i, 0)),
            pl.BlockSpec((d, PROJ_TN), lambda j, i: (0, jnp.minimum(j, kv_tiles - 1))),
            pl.BlockSpec((d, PROJ_TN), lambda j, i: (0, jnp.maximum(j - kv_tiles, 0))),
            wnext_in,
        ],
        out_specs=[pl.BlockSpec((PROJ_TM, PROJ_TN), lambda j, i: (i, j)), wnext_out],
        out_shape=[jax.ShapeDtypeStruct((n, w_kv.shape[1] + w_qg.shape[1]), jnp.bfloat16), wnext_shape],
        scratch_shapes=[pltpu.VMEM((d, PROJ_TN), jnp.bfloat16), _proj_scratch()],
        compiler_params=_params("arbitrary", "arbitrary"),
        name="attn_in",
    )(xb, w_kv, w_qg, w_next)


_EXP2_CLAMP = 64.0


def _softplus2(z):
    return jnp.maximum(z, jnp.log2(1.0 + jnp.exp2(jnp.minimum(z, _EXP2_CLAMP))))


def _attn_kernel(q_ref, k_ref, v_ref, m_ref, o_ref, c_ref, acc_ref):
    t = ATT_T
    dh = B_HEAD_DIM
    qi = pl.program_id(2)
    neg_tri = m_ref[...]

    def sweep(kbs, diagonal):
        heads = range(ATT_HEADS)
        starts = [pl.multiple_of(kb * t, t) for kb in kbs]
        chains = [(i, h) for i in range(len(kbs)) for h in heads]
        if any(diagonal):
            past = (lax.broadcasted_iota(jnp.int32, (t, t), 1)
                    < lax.broadcasted_iota(jnp.int32, (t, t), 0))
        z, suffix = {}, {}
        cs = [c_ref[h] for h in heads]
        accs = [acc_ref[h] for h in heads]

        def scores(i, h):
            z[i, h] = lax.dot_general(q_ref[:, h * dh:(h + 1) * dh],
                                      k_ref[pl.ds(starts[i], t), h * dh:(h + 1) * dh],
                                      (((1,), (1,)), ((), ())), preferred_element_type=jnp.float32)

        def suffix_sum(i, h):
            sp = _softplus2(z[i, h])
            if diagonal[i]:
                sp = jnp.where(past, sp, 0.0)
            suffix[i, h] = jnp.dot(sp.astype(jnp.bfloat16), neg_tri,
                                   preferred_element_type=jnp.float32)

        def values(i, h):
            w = jnp.exp2(z[i, h] + suffix[i, h] + cs[h])
            if diagonal[i]:
                w = jnp.where(past, w, 0.0)
            accs[h] = accs[h] + jnp.dot(w.astype(jnp.bfloat16),
                                        v_ref[pl.ds(starts[i], t), h * dh:(h + 1) * dh],
                                        preferred_element_type=jnp.float32)
            cs[h] = cs[h] + suffix[i, h][:, 0:1]

        n = len(chains)
        for step in range(n + 2 * ATT_LAG):
            if 0 <= step - ATT_LAG < n:
                suffix_sum(*chains[step - ATT_LAG])
            if step < n:
                scores(*chains[step])
            if 0 <= step - 2 * ATT_LAG < n:
                values(*chains[step - 2 * ATT_LAG])
        for h in heads:
            c_ref[h] = cs[h]
            acc_ref[h] = accs[h]

    c_ref[...] = jnp.zeros(c_ref.shape, c_ref.dtype)
    acc_ref[...] = jnp.zeros(acc_ref.shape, acc_ref.dtype)
    paired = qi % 2

    @pl.when(paired == 1)
    def _():
        sweep([qi, qi - 1], [True, False])

    @pl.when(paired == 0)
    def _():
        sweep([qi], [True])

    top = qi - paired

    @pl.loop(0, top // 2)
    def _(p):
        sweep([top - 1 - 2 * p, top - 2 - 2 * p], [False, False])

    for h in range(ATT_HEADS):
        o_ref[:, h * dh:(h + 1) * dh] = acc_ref[h].astype(o_ref.dtype)


def _attention(proj, neg_tri, bsz, seq):
    t = ATT_T
    nq = seq // t
    hw = ATT_HEADS * B_HEAD_DIM
    nh = B_WIDTH // hw
    return pl.pallas_call(
        _attn_kernel,
        grid=(bsz, nh, nq),
        in_specs=[
            pl.BlockSpec((t, hw), lambda b, h, i: (b * nq + i, 2 * nh + h)),
            pl.BlockSpec((seq, hw), lambda b, h, i: (b, h)),
            pl.BlockSpec((seq, hw), lambda b, h, i: (b, nh + h)),
            pl.BlockSpec((t, t), lambda b, h, i: (0, 0)),
        ],
        out_specs=pl.BlockSpec((t, hw), lambda b, h, i: (b * nq + i, h)),
        out_shape=jax.ShapeDtypeStruct((bsz * seq, B_WIDTH), jnp.bfloat16),
        scratch_shapes=[pltpu.VMEM((ATT_HEADS, t, 1), jnp.float32),
                        pltpu.VMEM((ATT_HEADS, t, B_HEAD_DIM), jnp.float32)],
        compiler_params=_params("parallel", "parallel", "arbitrary"),
        name="stick_breaking_attention",
    )(proj, proj, proj, neg_tri)


def _attn_out_kernel(o_ref, g_ref, x_ref, wo_ref, lg_ref, lb_ref, out_ref):
    s = (o_ref[...].astype(jnp.float32) * g_ref[...].astype(jnp.float32)).astype(jnp.bfloat16)
    y = jnp.dot(s, wo_ref[...], preferred_element_type=jnp.float32)
    out_ref[...] = _layer_norm(DN_ALPHA * x_ref[...] + y, lg_ref[...], lb_ref[...])


def _attn_out(o, proj, x, w_out, ln_g, ln_b):
    n, d = x.shape
    tm = ATTN_OUT_TM
    return pl.pallas_call(
        _attn_out_kernel,
        grid=(n // tm,),
        in_specs=[
            pl.BlockSpec((tm, B_WIDTH), lambda i: (i, 0)),
            pl.BlockSpec((tm, B_WIDTH), lambda i: (i, 3)),
            pl.BlockSpec((tm, d), lambda i: (i, 0)),
            _resident((B_WIDTH, d)),
            _resident((1, d)),
            _resident((1, d)),
        ],
        out_specs=pl.BlockSpec((tm, d), lambda i: (i, 0)),
        out_shape=jax.ShapeDtypeStruct((n, d), jnp.float32),
        compiler_params=_params("parallel"),
        name="attn_out",
    )(o, proj, x, w_out, ln_g, ln_b)


def kernel(x, a_w_in, a_b_in, a_vln_g, a_vln_b, a_w_s, a_b_s, a_w_out, kv_w, b_w_in, b_w_out, ln_g, ln_b):
    bsz, seq, d = x.shape
    n = bsz * seq
    bf16 = jnp.bfloat16
    n_a = a_w_in.shape[0]
    n_b = b_w_in.shape[0]

    assert n_a >= 1, "the attention layers read the bf16 stream written by the last gMLP layer"
    xf = x.reshape(n, d)
    xb = None
    for i in range(n_a):
        hact, w_out = _gmlp_in(xf, a_w_in[i], a_b_in[i][None, :], a_w_out[i])
        xf, xb = _gmlp_out(hact, xf, a_vln_g[i][None, :], a_vln_b[i][None, :], a_w_s[i],
                           a_b_s[i][:, :, None], w_out, ln_g[i][None, :], ln_b[i][None, :])

    idx = jnp.arange(ATT_T)
    neg_tri = jnp.where(idx[:, None] >= idx[None, :], -1.0, 0.0).astype(bf16)
    assert n_b == 1, "the fused k|v|q|gate projection supports exactly one attention layer"
    proj, w_out = _attn_in(xb, kv_w, b_w_in[0], b_w_out[0])
    o = _attention(proj, neg_tri, bsz, seq)
    xf = _attn_out(o, proj, xf, w_out, ln_g[n_a][None, :], ln_b[n_a][None, :])
    return xf.reshape(bsz, seq, d)
```

```python
import functools

import jax
import jax.numpy as jnp
from jax import lax
from jax.experimental import pallas as pl
from jax.experimental.pallas import tpu as pltpu

D_MODEL = 2048
DEPTH = 2
CHUNK = 128
A_WIDTH = 2 * D_MODEL
A_GROUPS = 8
A_GROUP_DIM = A_WIDTH // A_GROUPS
B_HEADS = 16
B_HEAD_DIM = D_MODEL // B_HEADS
B_WIDTH = B_HEADS * B_HEAD_DIM
DN_ALPHA = (2.0 * DEPTH) ** 0.25
LN_EPS = 1e-5

VMEM_LIMIT_BYTES = 56 * 1024 * 1024

PROJ_TM = 1024
PROJ_TN = 1024
PROJ_SUB = 256
PROJ_SLOTS = 3
WNEXT_BLOCKS = 32
MIX_TM = 256
ATTN_OUT_TM = 512
ATT_T = 256
ATT_HEADS = 8
ATT_TILES = 4
ATT_LAG = 2

_GELU_C = 0.7978845608028654
_LOG2_E = 1.4426950408889634


def _gelu_tanh(x):
    inner = x * (_GELU_C + (_GELU_C * 0.044715) * (x * x))
    return (0.5 * x) * (1.0 + jnp.tanh(inner))


def _silu(x):
    half = 0.5 * x
    return half * (1.0 + jnp.tanh(half))


def _layer_norm(z, g, b):
    mu = jnp.mean(z, axis=-1, keepdims=True)
    zc = z - mu
    var = jnp.mean(zc * zc, axis=-1, keepdims=True)
    return zc * lax.rsqrt(var + LN_EPS) * g + b


def _params(*semantics):
    return pltpu.CompilerParams(dimension_semantics=semantics, vmem_limit_bytes=VMEM_LIMIT_BYTES)


def _project_columns(x_ref, w_ref, o_ref, acc_ref, epilogue, b_ref=None):
    groups = w_ref.shape[1] // PROJ_SUB

    def product(n):
        cols = slice(n * PROJ_SUB, (n + 1) * PROJ_SUB)
        acc_ref[n % PROJ_SLOTS] = jnp.dot(x_ref[...], w_ref[:, cols].astype(jnp.bfloat16),
                                          preferred_element_type=jnp.float32)

    def finish(m):
        cols = slice(m * PROJ_SUB, (m + 1) * PROJ_SUB)
        h = acc_ref[m % PROJ_SLOTS]
        if b_ref is not None:
            h = h + b_ref[:, cols]
        o_ref[:, cols] = epilogue(h).astype(o_ref.dtype)

    lag = PROJ_SLOTS - 1
    for n in range(groups + lag):
        if n < groups:
            product(n)
        if n >= lag:
            finish(n - lag)


def _proj_scratch():
    return pltpu.VMEM((PROJ_SLOTS, PROJ_TM, PROJ_SUB), jnp.float32)


def _round_rows(step, src_ref, dst_ref, blocks):
    @pl.when(step < blocks)
    def _():
        dst_ref[...] = src_ref[...].astype(jnp.bfloat16)


def _round_rows_specs(w, blocks, step_of):
    rows = w.shape[0] // blocks
    index = lambda *ids: (jnp.minimum(step_of(*ids), blocks - 1), 0)
    spec = pl.BlockSpec((rows, w.shape[1]), index)
    return spec, spec, jax.ShapeDtypeStruct(w.shape, jnp.bfloat16)


def _gmlp_in_kernel(x_ref, w_ref, b_ref, wnext_ref, o_ref, wnext_bf_ref, xb_ref, acc_ref, *,
                    silu_from, wnext_blocks):
    j = pl.program_id(1)
    _round_rows(pl.program_id(0) * pl.num_programs(1) + j, wnext_ref, wnext_bf_ref, wnext_blocks)

    @pl.when(j == 0)
    def _():
        xb_ref[...] = x_ref[...].astype(jnp.bfloat16)

    @pl.when(j < silu_from)
    def _():
        _project_columns(xb_ref, w_ref, o_ref, acc_ref, _gelu_tanh, b_ref)

    @pl.when(j >= silu_from)
    def _():
        _project_columns(xb_ref, w_ref, o_ref, acc_ref, _silu, b_ref)


def _gmlp_in(x, w, b, w_next):
    n, d = x.shape
    width = w.shape[1]
    grid = (n // PROJ_TM, width // PROJ_TN)
    wnext_in, wnext_out, wnext_shape = _round_rows_specs(
        w_next, WNEXT_BLOCKS, lambda i, j: i * grid[1] + j)
    return pl.pallas_call(
        functools.partial(_gmlp_in_kernel, silu_from=2 * A_WIDTH // PROJ_TN,
                          wnext_blocks=WNEXT_BLOCKS),
        grid=grid,
        in_specs=[
            pl.BlockSpec((PROJ_TM, d), lambda i, j: (i, 0)),
            pl.BlockSpec((d, PROJ_TN), lambda i, j: (0, j)),
            pl.BlockSpec((1, PROJ_TN), lambda i, j: (0, j)),
            wnext_in,
        ],
        out_specs=[pl.BlockSpec((PROJ_TM, PROJ_TN), lambda i, j: (i, j)), wnext_out],
        out_shape=[jax.ShapeDtypeStruct((n, width), jnp.bfloat16), wnext_shape],
        scratch_shapes=[pltpu.VMEM((PROJ_TM, d), jnp.bfloat16), _proj_scratch()],
        compiler_params=_params("arbitrary", "arbitrary"),
        name="gmlp_in",
    )(x, w, b, w_next)


def _gmlp_out_kernel(u_ref, v_ref, g_ref, x_ref, vg_ref, vb_ref, ws_ref, bs_ref, wo_ref,
                     lg_ref, lb_ref, o_ref, ob_ref, s_ref):
    tm = u_ref.shape[0]
    vn = _layer_norm(v_ref[...].astype(jnp.float32), vg_ref[...], vb_ref[...]).astype(jnp.bfloat16)
    row = lax.broadcasted_iota(jnp.int32, (CHUNK, CHUNK), 0)
    col = lax.broadcasted_iota(jnp.int32, (CHUNK, CHUNK), 1)
    causal = col <= row
    for g in range(A_GROUPS):
        wt = jnp.where(causal, ws_ref[g], 0.0).astype(jnp.bfloat16)
        bias = bs_ref[g]
        cols = slice(g * A_GROUP_DIM, (g + 1) * A_GROUP_DIM)
        for c in range(tm // CHUNK):
            rows = slice(c * CHUNK, (c + 1) * CHUNK)
            mixed = jnp.dot(wt, vn[rows, cols], preferred_element_type=jnp.float32) + bias
            gate = u_ref[rows, cols].astype(jnp.float32) * g_ref[rows, cols].astype(jnp.float32)
            s_ref[rows, cols] = (gate * mixed).astype(jnp.bfloat16)
    y = jnp.dot(s_ref[...], wo_ref[...], preferred_element_type=jnp.float32)
    out = _layer_norm(DN_ALPHA * x_ref[...] + y, lg_ref[...], lb_ref[...])
    o_ref[...] = out
    ob_ref[...] = out.astype(jnp.bfloat16)


def _resident(shape):
    zeros = (0,) * len(shape)
    return pl.BlockSpec(shape, lambda i: zeros, pipeline_mode=pl.Buffered(1))


def _gmlp_out(hact, x, vln_g, vln_b, w_s, b_s, w_out, ln_g, ln_b):
    n, d = x.shape
    tm = MIX_TM
    return pl.pallas_call(
        _gmlp_out_kernel,
        grid=(n // tm,),
        in_specs=[
            pl.BlockSpec((tm, A_WIDTH), lambda i: (i, 0)),
            pl.BlockSpec((tm, A_WIDTH), lambda i: (i, 1)),
            pl.BlockSpec((tm, A_WIDTH), lambda i: (i, 2)),
            pl.BlockSpec((tm, d), lambda i: (i, 0)),
            _resident((1, A_WIDTH)),
            _resident((1, A_WIDTH)),
            _resident((A_GROUPS, CHUNK, CHUNK)),
            _resident((A_GROUPS, CHUNK, 1)),
            _resident((A_WIDTH, d)),
            _resident((1, d)),
            _resident((1, d)),
        ],
        out_specs=[
            pl.BlockSpec((tm, d), lambda i: (i, 0)),
            pl.BlockSpec((tm, d), lambda i: (i, 0)),
        ],
        out_shape=[
            jax.ShapeDtypeStruct((n, d), jnp.float32),
            jax.ShapeDtypeStruct((n, d), jnp.bfloat16),
        ],
        scratch_shapes=[pltpu.VMEM((tm, A_WIDTH), jnp.bfloat16)],
        compiler_params=_params("parallel"),
        name="gmlp_out",
    )(hact, hact, hact, x, vln_g, vln_b, w_s, b_s, w_out, ln_g, ln_b)


def _attn_in_kernel(x_ref, wkv_ref, wqg_ref, wnext_ref, o_ref, wnext_bf_ref, wb_ref, acc_ref, *,
                    q_from, gate_from, q_scale, wnext_blocks):
    j = pl.program_id(0)
    first_tile = pl.program_id(1) == 0
    _round_rows(j * pl.num_programs(1) + pl.program_id(1), wnext_ref, wnext_bf_ref, wnext_blocks)

    @pl.when(jnp.logical_and(first_tile, j < q_from))
    def _():
        wb_ref[...] = wkv_ref[...].astype(jnp.bfloat16)

    @pl.when(jnp.logical_and(first_tile, j >= q_from))
    def _():
        wb_ref[...] = wqg_ref[...].astype(jnp.bfloat16)

    @pl.when(j < q_from)
    def _():
        _project_columns(x_ref, wb_ref, o_ref, acc_ref, lambda h: h)

    @pl.when(jnp.logical_and(j >= q_from, j < gate_from))
    def _():
        _project_columns(x_ref, wb_ref, o_ref, acc_ref, lambda h: h * q_scale)

    @pl.when(j >= gate_from)
    def _():
        _project_columns(x_ref, wb_ref, o_ref, acc_ref, _silu)


def _attn_in(xb, w_kv, w_qg, w_next):
    n, d = xb.shape
    kv_tiles = w_kv.shape[1] // PROJ_TN
    qg_tiles = w_qg.shape[1] // PROJ_TN
    grid = (kv_tiles + qg_tiles, n // PROJ_TM)
    wnext_in, wnext_out, wnext_shape = _round_rows_specs(
        w_next, WNEXT_BLOCKS, lambda j, i: j * grid[1] + i)
    kernel = functools.partial(
        _attn_in_kernel,
        q_from=kv_tiles,
        gate_from=kv_tiles + qg_tiles // 2,
        q_scale=B_HEAD_DIM ** -0.5 * _LOG2_E,
        wnext_blocks=WNEXT_BLOCKS,
    )
    return pl.pallas_call(
        kernel,
        grid=grid,
        in_specs=[
            pl.BlockSpec((PROJ_TM, d), lambda j, i: (i, 0)),
            pl.BlockSpec((d, PROJ_TN), lambda j, i: (0, jnp.minimum(j, kv_tiles - 1))),
            pl.BlockSpec((d, PROJ_TN), lambda j, i: (0, jnp.maximum(j - kv_tiles, 0))),
            wnext_in,
        ],
        out_specs=[pl.BlockSpec((PROJ_TM, PROJ_TN), lambda j, i: (i, j)), wnext_out],
        out_shape=[jax.ShapeDtypeStruct((n, w_kv.shape[1] + w_qg.shape[1]), jnp.bfloat16), wnext_shape],
        scratch_shapes=[pltpu.VMEM((d, PROJ_TN), jnp.bfloat16), _proj_scratch()],
        compiler_params=_params("arbitrary", "arbitrary"),
        name="attn_in",
    )(xb, w_kv, w_qg, w_next)


_EXP2_CLAMP = 64.0


def _softplus2(z):
    return jnp.maximum(z, jnp.log2(1.0 + jnp.exp2(jnp.minimum(z, _EXP2_CLAMP))))


def _attn_kernel(q_ref, k_ref, v_ref, m_ref, o_ref, c_ref, acc_ref):
    t = ATT_T
    dh = B_HEAD_DIM
    qi = pl.program_id(2)
    neg_tri = m_ref[...]

    def sweep(kbs, diagonal):
        heads = range(ATT_HEADS)
        starts = [pl.multiple_of(kb * t, t) for kb in kbs]
        chains = [(i, h) for i in range(len(kbs)) for h in heads]
        if any(diagonal):
            past = (lax.broadcasted_iota(jnp.int32, (t, t), 1)
                    < lax.broadcasted_iota(jnp.int32, (t, t), 0))
        z, suffix = {}, {}
        cs = [c_ref[h] for h in heads]
        accs = [acc_ref[h] for h in heads]

        def scores(i, h):
            z[i, h] = lax.dot_general(q_ref[:, h * dh:(h + 1) * dh],
                                      k_ref[pl.ds(starts[i], t), h * dh:(h + 1) * dh],
                                      (((1,), (1,)), ((), ())), preferred_element_type=jnp.float32)

        def suffix_sum(i, h):
            sp = _softplus2(z[i, h])
            if diagonal[i]:
                sp = jnp.where(past, sp, 0.0)
            suffix[i, h] = jnp.dot(sp.astype(jnp.bfloat16), neg_tri,
                                   preferred_element_type=jnp.float32)

        def values(i, h):
            w = jnp.exp2(z[i, h] + suffix[i, h] + cs[h])
            if diagonal[i]:
                w = jnp.where(past, w, 0.0)
            accs[h] = accs[h] + jnp.dot(w.astype(jnp.bfloat16),
                                        v_ref[pl.ds(starts[i], t), h * dh:(h + 1) * dh],
                                        preferred_element_type=jnp.float32)
            cs[h] = cs[h] + suffix[i, h][:, 0:1]

        n = len(chains)
        for step in range(n + 2 * ATT_LAG):
            if 0 <= step - ATT_LAG < n:
                suffix_sum(*chains[step - ATT_LAG])
            if step < n:
                scores(*chains[step])
            if 0 <= step - 2 * ATT_LAG < n:
                values(*chains[step - 2 * ATT_LAG])
        for h in heads:
            c_ref[h] = cs[h]
            acc_ref[h] = accs[h]

    c_ref[...] = jnp.zeros(c_ref.shape, c_ref.dtype)
    acc_ref[...] = jnp.zeros(acc_ref.shape, acc_ref.dtype)
    first = qi % ATT_TILES + 1
    for r in range(1, ATT_TILES + 1):
        @pl.when(first == r)
        def _(r=r):
            sweep([qi - i for i in range(r)], [i == 0 for i in range(r)])

    top = qi + 1 - first

    @pl.loop(0, top // ATT_TILES)
    def _(p):
        sweep([top - 1 - ATT_TILES * p - i for i in range(ATT_TILES)], [False] * ATT_TILES)

    for h in range(ATT_HEADS):
        o_ref[:, h * dh:(h + 1) * dh] = acc_ref[h].astype(o_ref.dtype)


def _attention(proj, neg_tri, bsz, seq):
    t = ATT_T
    nq = seq // t
    hw = ATT_HEADS * B_HEAD_DIM
    nh = B_WIDTH // hw
    return pl.pallas_call(
        _attn_kernel,
        grid=(bsz, nh, nq),
        in_specs=[
            pl.BlockSpec((t, hw), lambda b, h, i: (b * nq + i, 2 * nh + h)),
            pl.BlockSpec((seq, hw), lambda b, h, i: (b, h)),
            pl.BlockSpec((seq, hw), lambda b, h, i: (b, nh + h)),
            pl.BlockSpec((t, t), lambda b, h, i: (0, 0)),
        ],
        out_specs=pl.BlockSpec((t, hw), lambda b, h, i: (b * nq + i, h)),
        out_shape=jax.ShapeDtypeStruct((bsz * seq, B_WIDTH), jnp.bfloat16),
        scratch_shapes=[pltpu.VMEM((ATT_HEADS, t, 1), jnp.float32),
                        pltpu.VMEM((ATT_HEADS, t, B_HEAD_DIM), jnp.float32)],
        compiler_params=_params("parallel", "parallel", "arbitrary"),
        name="stick_breaking_attention",
    )(proj, proj, proj, neg_tri)


def _attn_out_kernel(o_ref, g_ref, x_ref, wo_ref, lg_ref, lb_ref, out_ref):
    s = (o_ref[...].astype(jnp.float32) * g_ref[...].astype(jnp.float32)).astype(jnp.bfloat16)
    y = jnp.dot(s, wo_ref[...], preferred_element_type=jnp.float32)
    out_ref[...] = _layer_norm(DN_ALPHA * x_ref[...] + y, lg_ref[...], lb_ref[...])


def _attn_out(o, proj, x, w_out, ln_g, ln_b):
    n, d = x.shape
    tm = ATTN_OUT_TM
    return pl.pallas_call(
        _attn_out_kernel,
        grid=(n // tm,),
        in_specs=[
            pl.BlockSpec((tm, B_WIDTH), lambda i: (i, 0)),
            pl.BlockSpec((tm, B_WIDTH), lambda i: (i, 3)),
            pl.BlockSpec((tm, d), lambda i: (i, 0)),
            _resident((B_WIDTH, d)),
            _resident((1, d)),
            _resident((1, d)),
        ],
        out_specs=pl.BlockSpec((tm, d), lambda i: (i, 0)),
        out_shape=jax.ShapeDtypeStruct((n, d), jnp.float32),
        compiler_params=_params("parallel"),
        name="attn_out",
    )(o, proj, x, w_out, ln_g, ln_b)


def kernel(x, a_w_in, a_b_in, a_vln_g, a_vln_b, a_w_s, a_b_s, a_w_out, kv_w, b_w_in, b_w_out, ln_g, ln_b):
    bsz, seq, d = x.shape
    n = bsz * seq
    bf16 = jnp.bfloat16
    n_a = a_w_in.shape[0]
    n_b = b_w_in.shape[0]

    assert n_a >= 1, "the attention layers read the bf16 stream written by the last gMLP layer"
    xf = x.reshape(n, d)
    xb = None
    for i in range(n_a):
        hact, w_out = _gmlp_in(xf, a_w_in[i], a_b_in[i][None, :], a_w_out[i])
        xf, xb = _gmlp_out(hact, xf, a_vln_g[i][None, :], a_vln_b[i][None, :], a_w_s[i],
                           a_b_s[i][:, :, None], w_out, ln_g[i][None, :], ln_b[i][None, :])

    idx = jnp.arange(ATT_T)
    neg_tri = jnp.where(idx[:, None] >= idx[None, :], -1.0, 0.0).astype(bf16)
    assert n_b == 1, "the fused k|v|q|gate projection supports exactly one attention layer"
    proj, w_out = _attn_in(xb, kv_w, b_w_in[0], b_w_out[0])
    o = _attention(proj, neg_tri, bsz, seq)
    xf = _attn_out(o, proj, xf, w_out, ln_g[n_a][None, :], ln_b[n_a][None, :])
    return xf.reshape(bsz, seq, d)
```

```python
import functools

import jax
import jax.numpy as jnp
from jax import lax
from jax.experimental import pallas as pl
from jax.experimental.pallas import tpu as pltpu

D_MODEL = 2048
DEPTH = 2
CHUNK = 128
A_WIDTH = 2 * D_MODEL
A_GROUPS = 8
A_GROUP_DIM = A_WIDTH // A_GROUPS
B_HEADS = 16
B_HEAD_DIM = D_MODEL // B_HEADS
B_WIDTH = B_HEADS * B_HEAD_DIM
DN_ALPHA = (2.0 * DEPTH) ** 0.25
LN_EPS = 1e-5

VMEM_LIMIT_BYTES = 56 * 1024 * 1024

PROJ_TM = 1024
PROJ_TN = 1024
PROJ_SUB = 256
PROJ_SLOTS = 3
WNEXT_BLOCKS = 32
MIX_TM = 256
ATTN_OUT_TM = 512
ATT_T = 256
ATT_HEADS = 8
ATT_TILES = 6
ATT_LAG = 2

_GELU_C = 0.7978845608028654
_LOG2_E = 1.4426950408889634


def _gelu_tanh(x):
    inner = x * (_GELU_C + (_GELU_C * 0.044715) * (x * x))
    return (0.5 * x) * (1.0 + jnp.tanh(inner))


def _silu(x):
    half = 0.5 * x
    return half * (1.0 + jnp.tanh(half))


def _layer_norm(z, g, b):
    mu = jnp.mean(z, axis=-1, keepdims=True)
    zc = z - mu
    var = jnp.mean(zc * zc, axis=-1, keepdims=True)
    return zc * lax.rsqrt(var + LN_EPS) * g + b


def _params(*semantics):
    return pltpu.CompilerParams(dimension_semantics=semantics, vmem_limit_bytes=VMEM_LIMIT_BYTES)


def _project_columns(x_ref, w_ref, o_ref, acc_ref, epilogue, b_ref=None):
    groups = w_ref.shape[1] // PROJ_SUB

    def product(n):
        cols = slice(n * PROJ_SUB, (n + 1) * PROJ_SUB)
        acc_ref[n % PROJ_SLOTS] = jnp.dot(x_ref[...], w_ref[:, cols].astype(jnp.bfloat16),
                                          preferred_element_type=jnp.float32)

    def finish(m):
        cols = slice(m * PROJ_SUB, (m + 1) * PROJ_SUB)
        h = acc_ref[m % PROJ_SLOTS]
        if b_ref is not None:
            h = h + b_ref[:, cols]
        o_ref[:, cols] = epilogue(h).astype(o_ref.dtype)

    lag = PROJ_SLOTS - 1
    for n in range(groups + lag):
        if n < groups:
            product(n)
        if n >= lag:
            finish(n - lag)


def _proj_scratch():
    return pltpu.VMEM((PROJ_SLOTS, PROJ_TM, PROJ_SUB), jnp.float32)


def _round_rows(step, src_ref, dst_ref, blocks):
    @pl.when(step < blocks)
    def _():
        dst_ref[...] = src_ref[...].astype(jnp.bfloat16)


def _round_rows_specs(w, blocks, step_of):
    rows = w.shape[0] // blocks
    index = lambda *ids: (jnp.minimum(step_of(*ids), blocks - 1), 0)
    spec = pl.BlockSpec((rows, w.shape[1]), index)
    return spec, spec, jax.ShapeDtypeStruct(w.shape, jnp.bfloat16)


def _gmlp_in_kernel(x_ref, w_ref, b_ref, wnext_ref, o_ref, wnext_bf_ref, xb_ref, acc_ref, *,
                    silu_from, wnext_blocks):
    j = pl.program_id(1)
    _round_rows(pl.program_id(0) * pl.num_programs(1) + j, wnext_ref, wnext_bf_ref, wnext_blocks)

    @pl.when(j == 0)
    def _():
        xb_ref[...] = x_ref[...].astype(jnp.bfloat16)

    @pl.when(j < silu_from)
    def _():
        _project_columns(xb_ref, w_ref, o_ref, acc_ref, _gelu_tanh, b_ref)

    @pl.when(j >= silu_from)
    def _():
        _project_columns(xb_ref, w_ref, o_ref, acc_ref, _silu, b_ref)


def _gmlp_in(x, w, b, w_next):
    n, d = x.shape
    width = w.shape[1]
    grid = (n // PROJ_TM, width // PROJ_TN)
    wnext_in, wnext_out, wnext_shape = _round_rows_specs(
        w_next, WNEXT_BLOCKS, lambda i, j: i * grid[1] + j)
    return pl.pallas_call(
        functools.partial(_gmlp_in_kernel, silu_from=2 * A_WIDTH // PROJ_TN,
                          wnext_blocks=WNEXT_BLOCKS),
        grid=grid,
        in_specs=[
            pl.BlockSpec((PROJ_TM, d), lambda i, j: (i, 0)),
            pl.BlockSpec((d, PROJ_TN), lambda i, j: (0, j)),
            pl.BlockSpec((1, PROJ_TN), lambda i, j: (0, j)),
            wnext_in,
        ],
        out_specs=[pl.BlockSpec((PROJ_TM, PROJ_TN), lambda i, j: (i, j)), wnext_out],
        out_shape=[jax.ShapeDtypeStruct((n, width), jnp.bfloat16), wnext_shape],
        scratch_shapes=[pltpu.VMEM((PROJ_TM, d), jnp.bfloat16), _proj_scratch()],
        compiler_params=_params("arbitrary", "arbitrary"),
        name="gmlp_in",
    )(x, w, b, w_next)


def _gmlp_out_kernel(u_ref, v_ref, g_ref, x_ref, vg_ref, vb_ref, ws_ref, bs_ref, wo_ref,
                     lg_ref, lb_ref, o_ref, ob_ref, s_ref):
    tm = u_ref.shape[0]
    vn = _layer_norm(v_ref[...].astype(jnp.float32), vg_ref[...], vb_ref[...]).astype(jnp.bfloat16)
    row = lax.broadcasted_iota(jnp.int32, (CHUNK, CHUNK), 0)
    col = lax.broadcasted_iota(jnp.int32, (CHUNK, CHUNK), 1)
    causal = col <= row
    for g in range(A_GROUPS):
        wt = jnp.where(causal, ws_ref[g], 0.0).astype(jnp.bfloat16)
        bias = bs_ref[g]
        cols = slice(g * A_GROUP_DIM, (g + 1) * A_GROUP_DIM)
        for c in range(tm // CHUNK):
            rows = slice(c * CHUNK, (c + 1) * CHUNK)
            mixed = jnp.dot(wt, vn[rows, cols], preferred_element_type=jnp.float32) + bias
            gate = u_ref[rows, cols].astype(jnp.float32) * g_ref[rows, cols].astype(jnp.float32)
            s_ref[rows, cols] = (gate * mixed).astype(jnp.bfloat16)
    y = jnp.dot(s_ref[...], wo_ref[...], preferred_element_type=jnp.float32)
    out = _layer_norm(DN_ALPHA * x_ref[...] + y, lg_ref[...], lb_ref[...])
    o_ref[...] = out
    ob_ref[...] = out.astype(jnp.bfloat16)


def _resident(shape):
    zeros = (0,) * len(shape)
    return pl.BlockSpec(shape, lambda i: zeros, pipeline_mode=pl.Buffered(1))


def _gmlp_out(hact, x, vln_g, vln_b, w_s, b_s, w_out, ln_g, ln_b):
    n, d = x.shape
    tm = MIX_TM
    return pl.pallas_call(
        _gmlp_out_kernel,
        grid=(n // tm,),
        in_specs=[
            pl.BlockSpec((tm, A_WIDTH), lambda i: (i, 0)),
            pl.BlockSpec((tm, A_WIDTH), lambda i: (i, 1)),
            pl.BlockSpec((tm, A_WIDTH), lambda i: (i, 2)),
            pl.BlockSpec((tm, d), lambda i: (i, 0)),
            _resident((1, A_WIDTH)),
            _resident((1, A_WIDTH)),
            _resident((A_GROUPS, CHUNK, CHUNK)),
            _resident((A_GROUPS, CHUNK, 1)),
            _resident((A_WIDTH, d)),
            _resident((1, d)),
            _resident((1, d)),
        ],
        out_specs=[
            pl.BlockSpec((tm, d), lambda i: (i, 0)),
            pl.BlockSpec((tm, d), lambda i: (i, 0)),
        ],
        out_shape=[
            jax.ShapeDtypeStruct((n, d), jnp.float32),
            jax.ShapeDtypeStruct((n, d), jnp.bfloat16),
        ],
        scratch_shapes=[pltpu.VMEM((tm, A_WIDTH), jnp.bfloat16)],
        compiler_params=_params("parallel"),
        name="gmlp_out",
    )(hact, hact, hact, x, vln_g, vln_b, w_s, b_s, w_out, ln_g, ln_b)


def _attn_in_kernel(x_ref, wkv_ref, wqg_ref, wnext_ref, o_ref, wnext_bf_ref, wb_ref, acc_ref, *,
                    q_from, gate_from, q_scale, wnext_blocks):
    j = pl.program_id(0)
    first_tile = pl.program_id(1) == 0
    _round_rows(j * pl.num_programs(1) + pl.program_id(1), wnext_ref, wnext_bf_ref, wnext_blocks)

    @pl.when(jnp.logical_and(first_tile, j < q_from))
    def _():
        wb_ref[...] = wkv_ref[...].astype(jnp.bfloat16)

    @pl.when(jnp.logical_and(first_tile, j >= q_from))
    def _():
        wb_ref[...] = wqg_ref[...].astype(jnp.bfloat16)

    @pl.when(j < q_from)
    def _():
        _project_columns(x_ref, wb_ref, o_ref, acc_ref, lambda h: h)

    @pl.when(jnp.logical_and(j >= q_from, j < gate_from))
    def _():
        _project_columns(x_ref, wb_ref, o_ref, acc_ref, lambda h: h * q_scale)

    @pl.when(j >= gate_from)
    def _():
        _project_columns(x_ref, wb_ref, o_ref, acc_ref, _silu)


def _attn_in(xb, w_kv, w_qg, w_next):
    n, d = xb.shape
    kv_tiles = w_kv.shape[1] // PROJ_TN
    qg_tiles = w_qg.shape[1] // PROJ_TN
    grid = (kv_tiles + qg_tiles, n // PROJ_TM)
    wnext_in, wnext_out, wnext_shape = _round_rows_specs(
        w_next, WNEXT_BLOCKS, lambda j, i: j * grid[1] + i)
    kernel = functools.partial(
        _attn_in_kernel,
        q_from=kv_tiles,
        gate_from=kv_tiles + qg_tiles // 2,
        q_scale=B_HEAD_DIM ** -0.5 * _LOG2_E,
        wnext_blocks=WNEXT_BLOCKS,
    )
    return pl.pallas_call(
        kernel,
        grid=grid,
        in_specs=[
            pl.BlockSpec((PROJ_TM, d), lambda j, i: (i, 0)),
            pl.BlockSpec((d, PROJ_TN), lambda j, i: (0, jnp.minimum(j, kv_tiles - 1))),
            pl.BlockSpec((d, PROJ_TN), lambda j, i: (0, jnp.maximum(j - kv_tiles, 0))),
            wnext_in,
        ],
        out_specs=[pl.BlockSpec((PROJ_TM, PROJ_TN), lambda j, i: (i, j)), wnext_out],
        out_shape=[jax.ShapeDtypeStruct((n, w_kv.shape[1] + w_qg.shape[1]), jnp.bfloat16), wnext_shape],
        scratch_shapes=[pltpu.VMEM((d, PROJ_TN), jnp.bfloat16), _proj_scratch()],
        compiler_params=_params("arbitrary", "arbitrary"),
        name="attn_in",
    )(xb, w_kv, w_qg, w_next)


_EXP2_CLAMP = 64.0


def _softplus2(z):
    return jnp.maximum(z, jnp.log2(1.0 + jnp.exp2(jnp.minimum(z, _EXP2_CLAMP))))


def _attn_kernel(q_ref, k_ref, v_ref, m_ref, o_ref, c_ref, acc_ref):
    t = ATT_T
    dh = B_HEAD_DIM
    qi = pl.program_id(2)
    neg_tri = m_ref[...]

    def sweep(kbs, diagonal):
        heads = range(ATT_HEADS)
        starts = [pl.multiple_of(kb * t, t) for kb in kbs]
        chains = [(i, h) for i in range(len(kbs)) for h in heads]
        if any(diagonal):
            past = (lax.broadcasted_iota(jnp.int32, (t, t), 1)
                    < lax.broadcasted_iota(jnp.int32, (t, t), 0))
        z, suffix = {}, {}
        cs = [c_ref[h] for h in heads]
        accs = [acc_ref[h] for h in heads]

        def scores(i, h):
            z[i, h] = lax.dot_general(q_ref[:, h * dh:(h + 1) * dh],
                                      k_ref[pl.ds(starts[i], t), h * dh:(h + 1) * dh],
                                      (((1,), (1,)), ((), ())), preferred_element_type=jnp.float32)

        def suffix_sum(i, h):
            sp = _softplus2(z[i, h])
            if diagonal[i]:
                sp = jnp.where(past, sp, 0.0)
            suffix[i, h] = jnp.dot(sp.astype(jnp.bfloat16), neg_tri,
                                   preferred_element_type=jnp.float32)

        def values(i, h):
            w = jnp.exp2(z[i, h] + suffix[i, h] + cs[h])
            if diagonal[i]:
                w = jnp.where(past, w, 0.0)
            accs[h] = accs[h] + jnp.dot(w.astype(jnp.bfloat16),
                                        v_ref[pl.ds(starts[i], t), h * dh:(h + 1) * dh],
                                        preferred_element_type=jnp.float32)
            cs[h] = cs[h] + suffix[i, h][:, 0:1]

        n = len(chains)
        for step in range(n + 2 * ATT_LAG):
            if 0 <= step - ATT_LAG < n:
                suffix_sum(*chains[step - ATT_LAG])
            if step < n:
                scores(*chains[step])
            if 0 <= step - 2 * ATT_LAG < n:
                values(*chains[step - 2 * ATT_LAG])
        for h in heads:
            c_ref[h] = cs[h]
            acc_ref[h] = accs[h]

    c_ref[...] = jnp.zeros(c_ref.shape, c_ref.dtype)
    acc_ref[...] = jnp.zeros(acc_ref.shape, acc_ref.dtype)
    first = qi % ATT_TILES + 1
    for r in range(1, ATT_TILES + 1):
        @pl.when(first == r)
        def _(r=r):
            sweep([qi - i for i in range(r)], [i == 0 for i in range(r)])

    top = qi + 1 - first

    @pl.loop(0, top // ATT_TILES)
    def _(p):
        sweep([top - 1 - ATT_TILES * p - i for i in range(ATT_TILES)], [False] * ATT_TILES)

    for h in range(ATT_HEADS):
        o_ref[:, h * dh:(h + 1) * dh] = acc_ref[h].astype(o_ref.dtype)


def _attention(proj, neg_tri, bsz, seq):
    t = ATT_T
    nq = seq // t
    hw = ATT_HEADS * B_HEAD_DIM
    nh = B_WIDTH // hw
    return pl.pallas_call(
        _attn_kernel,
        grid=(bsz, nh, nq),
        in_specs=[
            pl.BlockSpec((t, hw), lambda b, h, i: (b * nq + i, 2 * nh + h)),
            pl.BlockSpec((seq, hw), lambda b, h, i: (b, h)),
            pl.BlockSpec((seq, hw), lambda b, h, i: (b, nh + h)),
            pl.BlockSpec((t, t), lambda b, h, i: (0, 0)),
        ],
        out_specs=pl.BlockSpec((t, hw), lambda b, h, i: (b * nq + i, h)),
        out_shape=jax.ShapeDtypeStruct((bsz * seq, B_WIDTH), jnp.bfloat16),
        scratch_shapes=[pltpu.VMEM((ATT_HEADS, t, 1), jnp.float32),
                        pltpu.VMEM((ATT_HEADS, t, B_HEAD_DIM), jnp.float32)],
        compiler_params=_params("parallel", "parallel", "arbitrary"),
        name="stick_breaking_attention",
    )(proj, proj, proj, neg_tri)


def _attn_out_kernel(o_ref, g_ref, x_ref, wo_ref, lg_ref, lb_ref, out_ref):
    s = (o_ref[...].astype(jnp.float32) * g_ref[...].astype(jnp.float32)).astype(jnp.bfloat16)
    y = jnp.dot(s, wo_ref[...], preferred_element_type=jnp.float32)
    out_ref[...] = _layer_norm(DN_ALPHA * x_ref[...] + y, lg_ref[...], lb_ref[...])


def _attn_out(o, proj, x, w_out, ln_g, ln_b):
    n, d = x.shape
    tm = ATTN_OUT_TM
    return pl.pallas_call(
        _attn_out_kernel,
        grid=(n // tm,),
        in_specs=[
            pl.BlockSpec((tm, B_WIDTH), lambda i: (i, 0)),
            pl.BlockSpec((tm, B_WIDTH), lambda i: (i, 3)),
            pl.BlockSpec((tm, d), lambda i: (i, 0)),
            _resident((B_WIDTH, d)),
            _resident((1, d)),
            _resident((1, d)),
        ],
        out_specs=pl.BlockSpec((tm, d), lambda i: (i, 0)),
        out_shape=jax.ShapeDtypeStruct((n, d), jnp.float32),
        compiler_params=_params("parallel"),
        name="attn_out",
    )(o, proj, x, w_out, ln_g, ln_b)


def kernel(x, a_w_in, a_b_in, a_vln_g, a_vln_b, a_w_s, a_b_s, a_w_out, kv_w, b_w_in, b_w_out, ln_g, ln_b):
    bsz, seq, d = x.shape
    n = bsz * seq
    bf16 = jnp.bfloat16
    n_a = a_w_in.shape[0]
    n_b = b_w_in.shape[0]

    assert n_a >= 1, "the attention layers read the bf16 stream written by the last gMLP layer"
    xf = x.reshape(n, d)
    xb = None
    for i in range(n_a):
        hact, w_out = _gmlp_in(xf, a_w_in[i], a_b_in[i][None, :], a_w_out[i])
        xf, xb = _gmlp_out(hact, xf, a_vln_g[i][None, :], a_vln_b[i][None, :], a_w_s[i],
                           a_b_s[i][:, :, None], w_out, ln_g[i][None, :], ln_b[i][None, :])

    idx = jnp.arange(ATT_T)
    neg_tri = jnp.where(idx[:, None] >= idx[None, :], -1.0, 0.0).astype(bf16)
    assert n_b == 1, "the fused k|v|q|gate projection supports exactly one attention layer"
    proj, w_out = _attn_in(xb, kv_w, b_w_in[0], b_w_out[0])
    o = _attention(proj, neg_tri, bsz, seq)
    xf = _attn_out(o, proj, xf, w_out, ln_g[n_a][None, :], ln_b[n_a][None, :])
    return xf.reshape(bsz, seq, d)
```

```python
import functools

import jax
import jax.numpy as jnp
from jax import lax
from jax.experimental import pallas as pl
from jax.experimental.pallas import tpu as pltpu

D_MODEL = 2048
DEPTH = 2
CHUNK = 128
A_WIDTH = 2 * D_MODEL
A_GROUPS = 8
A_GROUP_DIM = A_WIDTH // A_GROUPS
B_HEADS = 16
B_HEAD_DIM = D_MODEL // B_HEADS
B_WIDTH = B_HEADS * B_HEAD_DIM
DN_ALPHA = (2.0 * DEPTH) ** 0.25
LN_EPS = 1e-5

VMEM_LIMIT_BYTES = 56 * 1024 * 1024

PROJ_TM = 1024
ATTN_IN_TM = 2048
PROJ_TN = 1024
PROJ_SUB = 256
PROJ_SLOTS = 3
WNEXT_BLOCKS = 32
MIX_TM = 256
ATTN_OUT_TM = 512
ATT_T = 256
ATT_HEADS = 8
ATT_TILES = 4
ATT_LAG = 2

_GELU_C = 0.7978845608028654
_LOG2_E = 1.4426950408889634


def _gelu_tanh(x):
    inner = x * (_GELU_C + (_GELU_C * 0.044715) * (x * x))
    return (0.5 * x) * (1.0 + jnp.tanh(inner))


def _silu(x):
    half = 0.5 * x
    return half * (1.0 + jnp.tanh(half))


def _layer_norm(z, g, b):
    mu = jnp.mean(z, axis=-1, keepdims=True)
    zc = z - mu
    var = jnp.mean(zc * zc, axis=-1, keepdims=True)
    return zc * lax.rsqrt(var + LN_EPS) * g + b


def _params(*semantics):
    return pltpu.CompilerParams(dimension_semantics=semantics, vmem_limit_bytes=VMEM_LIMIT_BYTES)


def _project_columns(x_ref, w_ref, o_ref, acc_ref, epilogue, b_ref=None):
    groups = w_ref.shape[1] // PROJ_SUB

    def product(n):
        cols = slice(n * PROJ_SUB, (n + 1) * PROJ_SUB)
        acc_ref[n % PROJ_SLOTS] = jnp.dot(x_ref[...], w_ref[:, cols].astype(jnp.bfloat16),
                                          preferred_element_type=jnp.float32)

    def finish(m):
        cols = slice(m * PROJ_SUB, (m + 1) * PROJ_SUB)
        h = acc_ref[m % PROJ_SLOTS]
        if b_ref is not None:
            h = h + b_ref[:, cols]
        o_ref[:, cols] = epilogue(h).astype(o_ref.dtype)

    lag = PROJ_SLOTS - 1
    for n in range(groups + lag):
        if n < groups:
            product(n)
        if n >= lag:
            finish(n - lag)


def _proj_scratch(rows):
    return pltpu.VMEM((PROJ_SLOTS, rows, PROJ_SUB), jnp.float32)


def _round_rows(step, src_ref, dst_ref, start=0):
    @pl.when(jnp.logical_and(step >= start, step < start + WNEXT_BLOCKS))
    def _():
        dst_ref[...] = src_ref[...].astype(jnp.bfloat16)


def _round_rows_specs(w, step_of, start=0):
    rows = w.shape[0] // WNEXT_BLOCKS
    index = lambda *ids: (jnp.clip(step_of(*ids) - start, 0, WNEXT_BLOCKS - 1), 0)
    spec = pl.BlockSpec((rows, w.shape[1]), index)
    return spec, spec, jax.ShapeDtypeStruct(w.shape, jnp.bfloat16)


def _gmlp_in_kernel(x_ref, w_ref, b_ref, *rest, silu_from, later_weights):
    later_f32 = rest[:later_weights]
    o_ref = rest[later_weights]
    later_bf16 = rest[later_weights + 1:2 * later_weights + 1]
    xb_ref, acc_ref = rest[2 * later_weights + 1:]
    j = pl.program_id(1)
    step = pl.program_id(0) * pl.num_programs(1) + j
    for n, (src_ref, dst_ref) in enumerate(zip(later_f32, later_bf16)):
        _round_rows(step, src_ref, dst_ref, start=n * WNEXT_BLOCKS)

    @pl.when(j == 0)
    def _():
        xb_ref[...] = x_ref[...].astype(jnp.bfloat16)

    @pl.when(j < silu_from)
    def _():
        _project_columns(xb_ref, w_ref, o_ref, acc_ref, _gelu_tanh, b_ref)

    @pl.when(j >= silu_from)
    def _():
        _project_columns(xb_ref, w_ref, o_ref, acc_ref, _silu, b_ref)


def _gmlp_in(x, w, b, later):
    n, d = x.shape
    width = w.shape[1]
    grid = (n // PROJ_TM, width // PROJ_TN)
    assert len(later) * WNEXT_BLOCKS <= grid[0] * grid[1]
    later_specs = [_round_rows_specs(wl, lambda i, j: i * grid[1] + j, start=k * WNEXT_BLOCKS)
                   for k, wl in enumerate(later)]
    return pl.pallas_call(
        functools.partial(_gmlp_in_kernel, silu_from=2 * A_WIDTH // PROJ_TN,
                          later_weights=len(later)),
        grid=grid,
        in_specs=[
            pl.BlockSpec((PROJ_TM, d), lambda i, j: (i, 0)),
            pl.BlockSpec((d, PROJ_TN), lambda i, j: (0, j)),
            pl.BlockSpec((1, PROJ_TN), lambda i, j: (0, j)),
        ] + [spec[0] for spec in later_specs],
        out_specs=[pl.BlockSpec((PROJ_TM, PROJ_TN), lambda i, j: (i, j))]
        + [spec[1] for spec in later_specs],
        out_shape=[jax.ShapeDtypeStruct((n, width), jnp.bfloat16)]
        + [spec[2] for spec in later_specs],
        scratch_shapes=[pltpu.VMEM((PROJ_TM, d), jnp.bfloat16), _proj_scratch(PROJ_TM)],
        compiler_params=_params("arbitrary", "arbitrary"),
        name="gmlp_in",
    )(x, w, b, *later)


def _gmlp_out_kernel(u_ref, v_ref, g_ref, x_ref, vg_ref, vb_ref, ws_ref, bs_ref, wo_ref,
                     lg_ref, lb_ref, o_ref, ob_ref, s_ref):
    tm = u_ref.shape[0]
    vn = _layer_norm(v_ref[...].astype(jnp.float32), vg_ref[...], vb_ref[...]).astype(jnp.bfloat16)
    row = lax.broadcasted_iota(jnp.int32, (CHUNK, CHUNK), 0)
    col = lax.broadcasted_iota(jnp.int32, (CHUNK, CHUNK), 1)
    causal = col <= row
    for g in range(A_GROUPS):
        wt = jnp.where(causal, ws_ref[g], 0.0).astype(jnp.bfloat16)
        bias = bs_ref[g]
        cols = slice(g * A_GROUP_DIM, (g + 1) * A_GROUP_DIM)
        for c in range(tm // CHUNK):
            rows = slice(c * CHUNK, (c + 1) * CHUNK)
            mixed = jnp.dot(wt, vn[rows, cols], preferred_element_type=jnp.float32) + bias
            gate = u_ref[rows, cols].astype(jnp.float32) * g_ref[rows, cols].astype(jnp.float32)
            s_ref[rows, cols] = (gate * mixed).astype(jnp.bfloat16)
    y = jnp.dot(s_ref[...], wo_ref[...], preferred_element_type=jnp.float32)
    out = _layer_norm(DN_ALPHA * x_ref[...] + y, lg_ref[...], lb_ref[...])
    o_ref[...] = out
    ob_ref[...] = out.astype(jnp.bfloat16)


def _resident(shape):
    zeros = (0,) * len(shape)
    return pl.BlockSpec(shape, lambda i: zeros, pipeline_mode=pl.Buffered(1))


def _gmlp_out(hact, x, vln_g, vln_b, w_s, b_s, w_out, ln_g, ln_b):
    n, d = x.shape
    tm = MIX_TM
    return pl.pallas_call(
        _gmlp_out_kernel,
        grid=(n // tm,),
        in_specs=[
            pl.BlockSpec((tm, A_WIDTH), lambda i: (i, 0)),
            pl.BlockSpec((tm, A_WIDTH), lambda i: (i, 1)),
            pl.BlockSpec((tm, A_WIDTH), lambda i: (i, 2)),
            pl.BlockSpec((tm, d), lambda i: (i, 0)),
            _resident((1, A_WIDTH)),
            _resident((1, A_WIDTH)),
            _resident((A_GROUPS, CHUNK, CHUNK)),
            _resident((A_GROUPS, CHUNK, 1)),
            _resident((A_WIDTH, d)),
            _resident((1, d)),
            _resident((1, d)),
        ],
        out_specs=[
            pl.BlockSpec((tm, d), lambda i: (i, 0)),
            pl.BlockSpec((tm, d), lambda i: (i, 0)),
        ],
        out_shape=[
            jax.ShapeDtypeStruct((n, d), jnp.float32),
            jax.ShapeDtypeStruct((n, d), jnp.bfloat16),
        ],
        scratch_shapes=[pltpu.VMEM((tm, A_WIDTH), jnp.bfloat16)],
        compiler_params=_params("parallel"),
        name="gmlp_out",
    )(hact, hact, hact, x, vln_g, vln_b, w_s, b_s, w_out, ln_g, ln_b)


def _attn_in_kernel(x_ref, wkv_ref, wqg_ref, wnext_ref, o_ref, wnext_bf_ref, acc_ref, *,
                    q_from, gate_from, q_scale):
    j = pl.program_id(0)
    _round_rows(j * pl.num_programs(1) + pl.program_id(1), wnext_ref, wnext_bf_ref)

    @pl.when(j < q_from)
    def _():
        _project_columns(x_ref, wkv_ref, o_ref, acc_ref, lambda h: h)

    @pl.when(jnp.logical_and(j >= q_from, j < gate_from))
    def _():
        _project_columns(x_ref, wqg_ref, o_ref, acc_ref, lambda h: h * q_scale)

    @pl.when(j >= gate_from)
    def _():
        _project_columns(x_ref, wqg_ref, o_ref, acc_ref, _silu)


def _attn_in(xb, w_kv, w_qg, w_next):
    n, d = xb.shape
    tm = ATTN_IN_TM
    kv_tiles = w_kv.shape[1] // PROJ_TN
    qg_tiles = w_qg.shape[1] // PROJ_TN
    grid = (kv_tiles + qg_tiles, n // tm)
    assert WNEXT_BLOCKS <= grid[0] * grid[1]
    wnext_in, wnext_out, wnext_shape = _round_rows_specs(w_next, lambda j, i: j * grid[1] + i)
    kernel = functools.partial(
        _attn_in_kernel,
        q_from=kv_tiles,
        gate_from=kv_tiles + qg_tiles // 2,
        q_scale=B_HEAD_DIM ** -0.5 * _LOG2_E,
    )
    return pl.pallas_call(
        kernel,
        grid=grid,
        in_specs=[
            pl.BlockSpec((tm, d), lambda j, i: (i, 0)),
            pl.BlockSpec((d, PROJ_TN), lambda j, i: (0, jnp.minimum(j, kv_tiles - 1))),
            pl.BlockSpec((d, PROJ_TN), lambda j, i: (0, jnp.maximum(j - kv_tiles, 0))),
            wnext_in,
        ],
        out_specs=[pl.BlockSpec((tm, PROJ_TN), lambda j, i: (i, j)), wnext_out],
        out_shape=[jax.ShapeDtypeStruct((n, w_kv.shape[1] + w_qg.shape[1]), jnp.bfloat16), wnext_shape],
        scratch_shapes=[_proj_scratch(tm)],
        compiler_params=_params("arbitrary", "arbitrary"),
        name="attn_in",
    )(xb, w_kv, w_qg, w_next)


_EXP2_CLAMP = 64.0


def _softplus2(z):
    return jnp.maximum(z, jnp.log2(1.0 + jnp.exp2(jnp.minimum(z, _EXP2_CLAMP))))


def _attn_kernel(q_ref, k_ref, v_ref, m_ref, o_ref, c_ref, acc_ref):
    t = ATT_T
    dh = B_HEAD_DIM
    qi = pl.program_id(2)
    neg_tri = m_ref[...]

    def sweep(kbs, diagonal):
        heads = range(ATT_HEADS)
        starts = [pl.multiple_of(kb * t, t) for kb in kbs]
        chains = [(i, h) for i in range(len(kbs)) for h in heads]
        if any(diagonal):
            past = (lax.broadcasted_iota(jnp.int32, (t, t), 1)
                    < lax.broadcasted_iota(jnp.int32, (t, t), 0))
        z, suffix = {}, {}
        cs = [c_ref[h] for h in heads]
        accs = [acc_ref[h] for h in heads]

        def scores(i, h):
            z[i, h] = lax.dot_general(q_ref[:, h * dh:(h + 1) * dh],
                                      k_ref[pl.ds(starts[i], t), h * dh:(h + 1) * dh],
                                      (((1,), (1,)), ((), ())), preferred_element_type=jnp.float32)

        def suffix_sum(i, h):
            sp = _softplus2(z[i, h])
            if diagonal[i]:
                sp = jnp.where(past, sp, 0.0)
            suffix[i, h] = jnp.dot(sp.astype(jnp.bfloat16), neg_tri,
                                   preferred_element_type=jnp.float32)

        def values(i, h):
            w = jnp.exp2(z[i, h] + suffix[i, h] + cs[h])
            if diagonal[i]:
                w = jnp.where(past, w, 0.0)
            accs[h] = accs[h] + jnp.dot(w.astype(jnp.bfloat16),
                                        v_ref[pl.ds(starts[i], t), h * dh:(h + 1) * dh],
                                        preferred_element_type=jnp.float32)
            cs[h] = cs[h] + suffix[i, h][:, 0:1]

        n = len(chains)
        for step in range(n + 2 * ATT_LAG):
            if 0 <= step - ATT_LAG < n:
                suffix_sum(*chains[step - ATT_LAG])
            if step < n:
                scores(*chains[step])
            if 0 <= step - 2 * ATT_LAG < n:
                values(*chains[step - 2 * ATT_LAG])
        for h in heads:
            c_ref[h] = cs[h]
            acc_ref[h] = accs[h]

    c_ref[...] = jnp.zeros(c_ref.shape, c_ref.dtype)
    acc_ref[...] = jnp.zeros(acc_ref.shape, acc_ref.dtype)
    first = qi % ATT_TILES + 1
    for r in range(1, ATT_TILES + 1):
        @pl.when(first == r)
        def _(r=r):
            sweep([qi - i for i in range(r)], [i == 0 for i in range(r)])

    top = qi + 1 - first

    @pl.loop(0, top // ATT_TILES)
    def _(p):
        sweep([top - 1 - ATT_TILES * p - i for i in range(ATT_TILES)], [False] * ATT_TILES)

    for h in range(ATT_HEADS):
        o_ref[:, h * dh:(h + 1) * dh] = acc_ref[h].astype(o_ref.dtype)


def _attention(proj, neg_tri, bsz, seq):
    t = ATT_T
    nq = seq // t
    hw = ATT_HEADS * B_HEAD_DIM
    nh = B_WIDTH // hw
    return pl.pallas_call(
        _attn_kernel,
        grid=(bsz, nh, nq),
        in_specs=[
            pl.BlockSpec((t, hw), lambda b, h, i: (b * nq + i, 2 * nh + h)),
            pl.BlockSpec((seq, hw), lambda b, h, i: (b, h)),
            pl.BlockSpec((seq, hw), lambda b, h, i: (b, nh + h)),
            pl.BlockSpec((t, t), lambda b, h, i: (0, 0)),
        ],
        out_specs=pl.BlockSpec((t, hw), lambda b, h, i: (b * nq + i, h)),
        out_shape=jax.ShapeDtypeStruct((bsz * seq, B_WIDTH), jnp.bfloat16),
        scratch_shapes=[pltpu.VMEM((ATT_HEADS, t, 1), jnp.float32),
                        pltpu.VMEM((ATT_HEADS, t, B_HEAD_DIM), jnp.float32)],
        compiler_params=_params("parallel", "parallel", "arbitrary"),
        name="stick_breaking_attention",
    )(proj, proj, proj, neg_tri)


def _attn_out_kernel(o_ref, g_ref, x_ref, wo_ref, lg_ref, lb_ref, out_ref):
    s = (o_ref[...].astype(jnp.float32) * g_ref[...].astype(jnp.float32)).astype(jnp.bfloat16)
    y = jnp.dot(s, wo_ref[...], preferred_element_type=jnp.float32)
    out_ref[...] = _layer_norm(DN_ALPHA * x_ref[...] + y, lg_ref[...], lb_ref[...])


def _attn_out(o, proj, x, w_out, ln_g, ln_b):
    n, d = x.shape
    tm = ATTN_OUT_TM
    return pl.pallas_call(
        _attn_out_kernel,
        grid=(n // tm,),
        in_specs=[
            pl.BlockSpec((tm, B_WIDTH), lambda i: (i, 0)),
            pl.BlockSpec((tm, B_WIDTH), lambda i: (i, 3)),
            pl.BlockSpec((tm, d), lambda i: (i, 0)),
            _resident((B_WIDTH, d)),
            _resident((1, d)),
            _resident((1, d)),
        ],
        out_specs=pl.BlockSpec((tm, d), lambda i: (i, 0)),
        out_shape=jax.ShapeDtypeStruct((n, d), jnp.float32),
        compiler_params=_params("parallel"),
        name="attn_out",
    )(o, proj, x, w_out, ln_g, ln_b)


def kernel(x, a_w_in, a_b_in, a_vln_g, a_vln_b, a_w_s, a_b_s, a_w_out, kv_w, b_w_in, b_w_out, ln_g, ln_b):
    bsz, seq, d = x.shape
    n = bsz * seq
    bf16 = jnp.bfloat16
    n_a = a_w_in.shape[0]
    n_b = b_w_in.shape[0]

    assert n_a >= 1, "the attention layers read the bf16 stream written by the last gMLP layer"
    assert n_b == 1, "the fused k|v|q|gate projection supports exactly one attention layer"
    xf = x.reshape(n, d)
    xb = None
    for i in range(n_a):
        later = [a_w_out[i]] + ([kv_w, b_w_in[0]] if i == n_a - 1 else [])
        hact, w_out, *attn_w = _gmlp_in(xf, a_w_in[i], a_b_in[i][None, :], later)
        xf, xb = _gmlp_out(hact, xf, a_vln_g[i][None, :], a_vln_b[i][None, :], a_w_s[i],
                           a_b_s[i][:, :, None], w_out, ln_g[i][None, :], ln_b[i][None, :])

    idx = jnp.arange(ATT_T)
    neg_tri = jnp.where(idx[:, None] >= idx[None, :], -1.0, 0.0).astype(bf16)
    proj, w_out = _attn_in(xb, attn_w[0], attn_w[1], b_w_out[0])
    o = _attention(proj, neg_tri, bsz, seq)
    xf = _attn_out(o, proj, xf, w_out, ln_g[n_a][None, :], ln_b[n_a][None, :])
    return xf.reshape(bsz, seq, d)
```

```python
import functools

import jax
import jax.numpy as jnp
from jax import lax
from jax.experimental import pallas as pl
from jax.experimental.pallas import tpu as pltpu

D_MODEL = 2048
DEPTH = 2
CHUNK = 128
A_WIDTH = 2 * D_MODEL
A_GROUPS = 8
A_GROUP_DIM = A_WIDTH // A_GROUPS
B_HEADS = 16
B_HEAD_DIM = D_MODEL // B_HEADS
B_WIDTH = B_HEADS * B_HEAD_DIM
DN_ALPHA = (2.0 * DEPTH) ** 0.25
LN_EPS = 1e-5

VMEM_LIMIT_BYTES = 56 * 1024 * 1024

PROJ_TM = 1024
PROJ_TN = 1024
PROJ_SUB = 256
PROJ_SLOTS = 3
WNEXT_BLOCKS = 32
MIX_TM = 256
ATTN_OUT_TM = 512
ATT_T = 256
ATT_HEADS = 8
ATT_TILES = 4
ATT_LAG = 2

_GELU_C = 0.7978845608028654
_LOG2_E = 1.4426950408889634


def _gelu_tanh(x):
    inner = x * (_GELU_C + (_GELU_C * 0.044715) * (x * x))
    return (0.5 * x) * (1.0 + jnp.tanh(inner))


def _silu(x):
    half = 0.5 * x
    return half * (1.0 + jnp.tanh(half))


def _layer_norm(z, g, b):
    mu = jnp.mean(z, axis=-1, keepdims=True)
    zc = z - mu
    var = jnp.mean(zc * zc, axis=-1, keepdims=True)
    return zc * lax.rsqrt(var + LN_EPS) * g + b


def _params(*semantics):
    return pltpu.CompilerParams(dimension_semantics=semantics, vmem_limit_bytes=VMEM_LIMIT_BYTES)


def _project_columns(x_ref, w_ref, o_ref, acc_ref, epilogue, b_ref=None):
    groups = w_ref.shape[1] // PROJ_SUB

    def product(n):
        cols = slice(n * PROJ_SUB, (n + 1) * PROJ_SUB)
        acc_ref[n % PROJ_SLOTS] = jnp.dot(x_ref[...], w_ref[:, cols].astype(jnp.bfloat16),
                                          preferred_element_type=jnp.float32)

    def finish(m):
        cols = slice(m * PROJ_SUB, (m + 1) * PROJ_SUB)
        h = acc_ref[m % PROJ_SLOTS]
        if b_ref is not None:
            h = h + b_ref[:, cols]
        o_ref[:, cols] = epilogue(h).astype(o_ref.dtype)

    lag = PROJ_SLOTS - 1
    for n in range(groups + lag):
        if n < groups:
            product(n)
        if n >= lag:
            finish(n - lag)


def _proj_scratch():
    return pltpu.VMEM((PROJ_SLOTS, PROJ_TM, PROJ_SUB), jnp.float32)


def _round_rows(step, src_ref, dst_ref, blocks):
    @pl.when(step < blocks)
    def _():
        dst_ref[...] = src_ref[...].astype(jnp.bfloat16)


def _round_rows_specs(w, blocks, step_of):
    rows = w.shape[0] // blocks
    index = lambda *ids: (jnp.minimum(step_of(*ids), blocks - 1), 0)
    spec = pl.BlockSpec((rows, w.shape[1]), index)
    return spec, spec, jax.ShapeDtypeStruct(w.shape, jnp.bfloat16)


def _gmlp_in_kernel(x_ref, w_ref, b_ref, wnext_ref, o_ref, wnext_bf_ref, xb_ref, acc_ref, *,
                    silu_from, wnext_blocks):
    j = pl.program_id(1)
    _round_rows(pl.program_id(0) * pl.num_programs(1) + j, wnext_ref, wnext_bf_ref, wnext_blocks)

    @pl.when(j == 0)
    def _():
        xb_ref[...] = x_ref[...].astype(jnp.bfloat16)

    @pl.when(j < silu_from)
    def _():
        _project_columns(xb_ref, w_ref, o_ref, acc_ref, _gelu_tanh, b_ref)

    @pl.when(j >= silu_from)
    def _():
        _project_columns(xb_ref, w_ref, o_ref, acc_ref, _silu, b_ref)


def _gmlp_in(x, w, b, w_next):
    n, d = x.shape
    width = w.shape[1]
    grid = (n // PROJ_TM, width // PROJ_TN)
    wnext_in, wnext_out, wnext_shape = _round_rows_specs(
        w_next, WNEXT_BLOCKS, lambda i, j: i * grid[1] + j)
    return pl.pallas_call(
        functools.partial(_gmlp_in_kernel, silu_from=2 * A_WIDTH // PROJ_TN,
                          wnext_blocks=WNEXT_BLOCKS),
        grid=grid,
        in_specs=[
            pl.BlockSpec((PROJ_TM, d), lambda i, j: (i, 0)),
            pl.BlockSpec((d, PROJ_TN), lambda i, j: (0, j)),
            pl.BlockSpec((1, PROJ_TN), lambda i, j: (0, j)),
            wnext_in,
        ],
        out_specs=[pl.BlockSpec((PROJ_TM, PROJ_TN), lambda i, j: (i, j)), wnext_out],
        out_shape=[jax.ShapeDtypeStruct((n, width), jnp.bfloat16), wnext_shape],
        scratch_shapes=[pltpu.VMEM((PROJ_TM, d), jnp.bfloat16), _proj_scratch()],
        compiler_params=_params("arbitrary", "arbitrary"),
        name="gmlp_in",
    )(x, w, b, w_next)


def _gmlp_out_kernel(u_ref, v_ref, g_ref, x_ref, vg_ref, vb_ref, ws_ref, bs_ref, wo_ref,
                     lg_ref, lb_ref, o_ref, ob_ref, s_ref):
    tm = u_ref.shape[0]
    vn = _layer_norm(v_ref[...].astype(jnp.float32), vg_ref[...], vb_ref[...]).astype(jnp.bfloat16)
    row = lax.broadcasted_iota(jnp.int32, (CHUNK, CHUNK), 0)
    col = lax.broadcasted_iota(jnp.int32, (CHUNK, CHUNK), 1)
    causal = col <= row
    for g in range(A_GROUPS):
        wt = jnp.where(causal, ws_ref[g], 0.0).astype(jnp.bfloat16)
        bias = bs_ref[g]
        cols = slice(g * A_GROUP_DIM, (g + 1) * A_GROUP_DIM)
        for c in range(tm // CHUNK):
            rows = slice(c * CHUNK, (c + 1) * CHUNK)
            mixed = jnp.dot(wt, vn[rows, cols], preferred_element_type=jnp.float32) + bias
            gate = u_ref[rows, cols].astype(jnp.float32) * g_ref[rows, cols].astype(jnp.float32)
            s_ref[rows, cols] = (gate * mixed).astype(jnp.bfloat16)
    y = jnp.dot(s_ref[...], wo_ref[...], preferred_element_type=jnp.float32)
    out = _layer_norm(DN_ALPHA * x_ref[...] + y, lg_ref[...], lb_ref[...])
    o_ref[...] = out
    ob_ref[...] = out.astype(jnp.bfloat16)


def _resident(shape):
    zeros = (0,) * len(shape)
    return pl.BlockSpec(shape, lambda i: zeros, pipeline_mode=pl.Buffered(1))


def _gmlp_out(hact, x, vln_g, vln_b, w_s, b_s, w_out, ln_g, ln_b):
    n, d = x.shape
    tm = MIX_TM
    return pl.pallas_call(
        _gmlp_out_kernel,
        grid=(n // tm,),
        in_specs=[
            pl.BlockSpec((tm, A_WIDTH), lambda i: (i, 0)),
            pl.BlockSpec((tm, A_WIDTH), lambda i: (i, 1)),
            pl.BlockSpec((tm, A_WIDTH), lambda i: (i, 2)),
            pl.BlockSpec((tm, d), lambda i: (i, 0)),
            _resident((1, A_WIDTH)),
            _resident((1, A_WIDTH)),
            _resident((A_GROUPS, CHUNK, CHUNK)),
            _resident((A_GROUPS, CHUNK, 1)),
            _resident((A_WIDTH, d)),
            _resident((1, d)),
            _resident((1, d)),
        ],
        out_specs=[
            pl.BlockSpec((tm, d), lambda i: (i, 0)),
            pl.BlockSpec((tm, d), lambda i: (i, 0)),
        ],
        out_shape=[
            jax.ShapeDtypeStruct((n, d), jnp.float32),
            jax.ShapeDtypeStruct((n, d), jnp.bfloat16),
        ],
        scratch_shapes=[pltpu.VMEM((tm, A_WIDTH), jnp.bfloat16)],
        compiler_params=_params("parallel"),
        name="gmlp_out",
    )(hact, hact, hact, x, vln_g, vln_b, w_s, b_s, w_out, ln_g, ln_b)


def _attn_in_kernel(x_ref, wkv_ref, wqg_ref, wnext_ref, o_ref, wnext_bf_ref, wb_ref, acc_ref, *,
                    q_from, gate_from, q_scale, wnext_blocks):
    j = pl.program_id(0)
    first_tile = pl.program_id(1) == 0
    _round_rows(j * pl.num_programs(1) + pl.program_id(1), wnext_ref, wnext_bf_ref, wnext_blocks)

    @pl.when(jnp.logical_and(first_tile, j < q_from))
    def _():
        wb_ref[...] = wkv_ref[...].astype(jnp.bfloat16)

    @pl.when(jnp.logical_and(first_tile, j >= q_from))
    def _():
        wb_ref[...] = wqg_ref[...].astype(jnp.bfloat16)

    @pl.when(j < q_from)
    def _():
        _project_columns(x_ref, wb_ref, o_ref, acc_ref, lambda h: h)

    @pl.when(jnp.logical_and(j >= q_from, j < gate_from))
    def _():
        _project_columns(x_ref, wb_ref, o_ref, acc_ref, lambda h: h * q_scale)

    @pl.when(j >= gate_from)
    def _():
        _project_columns(x_ref, wb_ref, o_ref, acc_ref, _silu)


def _attn_in(xb, w_kv, w_qg, w_next):
    n, d = xb.shape
    kv_tiles = w_kv.shape[1] // PROJ_TN
    qg_tiles = w_qg.shape[1] // PROJ_TN
    grid = (kv_tiles + qg_tiles, n // PROJ_TM)
    wnext_in, wnext_out, wnext_shape = _round_rows_specs(
        w_next, WNEXT_BLOCKS, lambda j, i: j * grid[1] + i)
    kernel = functools.partial(
        _attn_in_kernel,
        q_from=kv_tiles,
        gate_from=kv_tiles + qg_tiles // 2,
        q_scale=B_HEAD_DIM ** -0.5 * _LOG2_E,
        wnext_blocks=WNEXT_BLOCKS,
    )
    return pl.pallas_call(
        kernel,
        grid=grid,
        in_specs=[
            pl.BlockSpec((PROJ_TM, d), lambda j, i: (i, 0)),
            pl.BlockSpec((d, PROJ_TN), lambda j, i: (0, jnp.minimum(j, kv_tiles - 1))),
            pl.BlockSpec((d, PROJ_TN), lambda j, i: (0, jnp.maximum(j - kv_tiles, 0))),
            wnext_in,
        ],
        out_specs=[pl.BlockSpec((PROJ_TM, PROJ_TN), lambda j, i: (i, j)), wnext_out],
        out_shape=[jax.ShapeDtypeStruct((n, w_kv.shape[1] + w_qg.shape[1]), jnp.bfloat16), wnext_shape],
        scratch_shapes=[pltpu.VMEM((d, PROJ_TN), jnp.bfloat16), _proj_scratch()],
        compiler_params=_params("arbitrary", "arbitrary"),
        name="attn_in",
    )(xb, w_kv, w_qg, w_next)


_EXP2_CLAMP = 64.0


def _softplus2(z):
    return jnp.maximum(z, jnp.log2(1.0 + jnp.exp2(jnp.minimum(z, _EXP2_CLAMP))))


def _attn_kernel(q_ref, k_ref, v_ref, m_ref, o_ref, c_ref, acc_ref):
    t = ATT_T
    dh = B_HEAD_DIM
    qi = pl.program_id(2)
    neg_tri = m_ref[...]

    def sweep(kbs, diagonal, fresh=False):
        heads = range(ATT_HEADS)
        starts = [pl.multiple_of(kb * t, t) for kb in kbs]
        chains = [(i, h) for i in range(len(kbs)) for h in heads]
        if any(diagonal):
            past = (lax.broadcasted_iota(jnp.int32, (t, t), 1)
                    < lax.broadcasted_iota(jnp.int32, (t, t), 0))
        z, suffix = {}, {}
        if fresh:
            cs = [jnp.zeros(c_ref.shape[1:], jnp.float32) for h in heads]
            accs = [jnp.zeros(acc_ref.shape[1:], jnp.float32) for h in heads]
        else:
            cs = [c_ref[h] for h in heads]
            accs = [acc_ref[h] for h in heads]

        def scores(i, h):
            z[i, h] = lax.dot_general(q_ref[:, h * dh:(h + 1) * dh],
                                      k_ref[pl.ds(starts[i], t), h * dh:(h + 1) * dh],
                                      (((1,), (1,)), ((), ())), preferred_element_type=jnp.float32)

        def suffix_sum(i, h):
            sp = _softplus2(z[i, h])
            if diagonal[i]:
                sp = jnp.where(past, sp, 0.0)
            suffix[i, h] = jnp.dot(sp.astype(jnp.bfloat16), neg_tri,
                                   preferred_element_type=jnp.float32)

        def values(i, h):
            w = jnp.exp2(z[i, h] + suffix[i, h] + cs[h])
            if diagonal[i]:
                w = jnp.where(past, w, 0.0)
            accs[h] = accs[h] + jnp.dot(w.astype(jnp.bfloat16),
                                        v_ref[pl.ds(starts[i], t), h * dh:(h + 1) * dh],
                                        preferred_element_type=jnp.float32)
            cs[h] = cs[h] + suffix[i, h][:, 0:1]

        n = len(chains)
        for step in range(n + 2 * ATT_LAG):
            if 0 <= step - ATT_LAG < n:
                suffix_sum(*chains[step - ATT_LAG])
            if step < n:
                scores(*chains[step])
            if 0 <= step - 2 * ATT_LAG < n:
                values(*chains[step - 2 * ATT_LAG])
        for h in heads:
            c_ref[h] = cs[h]
            acc_ref[h] = accs[h]

    first = qi % ATT_TILES + 1
    for r in range(1, ATT_TILES + 1):
        @pl.when(first == r)
        def _(r=r):
            sweep([qi - i for i in range(r)], [i == 0 for i in range(r)], fresh=True)

    top = qi + 1 - first

    @pl.loop(0, top // ATT_TILES)
    def _(p):
        sweep([top - 1 - ATT_TILES * p - i for i in range(ATT_TILES)], [False] * ATT_TILES)

    for h in range(ATT_HEADS):
        o_ref[:, h * dh:(h + 1) * dh] = acc_ref[h].astype(o_ref.dtype)


def _attention(proj, neg_tri, bsz, seq):
    t = ATT_T
    nq = seq // t
    hw = ATT_HEADS * B_HEAD_DIM
    nh = B_WIDTH // hw
    return pl.pallas_call(
        _attn_kernel,
        grid=(bsz, nh, nq),
        in_specs=[
            pl.BlockSpec((t, hw), lambda b, h, i: (b * nq + i, 2 * nh + h)),
            pl.BlockSpec((seq, hw), lambda b, h, i: (b, h)),
            pl.BlockSpec((seq, hw), lambda b, h, i: (b, nh + h)),
            pl.BlockSpec((t, t), lambda b, h, i: (0, 0)),
        ],
        out_specs=pl.BlockSpec((t, hw), lambda b, h, i: (b * nq + i, h)),
        out_shape=jax.ShapeDtypeStruct((bsz * seq, B_WIDTH), jnp.bfloat16),
        scratch_shapes=[pltpu.VMEM((ATT_HEADS, t, 1), jnp.float32),
                        pltpu.VMEM((ATT_HEADS, t, B_HEAD_DIM), jnp.float32)],
        compiler_params=_params("parallel", "parallel", "arbitrary"),
        name="stick_breaking_attention",
    )(proj, proj, proj, neg_tri)


def _attn_out_kernel(o_ref, g_ref, x_ref, wo_ref, lg_ref, lb_ref, out_ref):
    s = (o_ref[...].astype(jnp.float32) * g_ref[...].astype(jnp.float32)).astype(jnp.bfloat16)
    y = jnp.dot(s, wo_ref[...], preferred_element_type=jnp.float32)
    out_ref[...] = _layer_norm(DN_ALPHA * x_ref[...] + y, lg_ref[...], lb_ref[...])


def _attn_out(o, proj, x, w_out, ln_g, ln_b):
    n, d = x.shape
    tm = ATTN_OUT_TM
    return pl.pallas_call(
        _attn_out_kernel,
        grid=(n // tm,),
        in_specs=[
            pl.BlockSpec((tm, B_WIDTH), lambda i: (i, 0)),
            pl.BlockSpec((tm, B_WIDTH), lambda i: (i, 3)),
            pl.BlockSpec((tm, d), lambda i: (i, 0)),
            _resident((B_WIDTH, d)),
            _resident((1, d)),
            _resident((1, d)),
        ],
        out_specs=pl.BlockSpec((tm, d), lambda i: (i, 0)),
        out_shape=jax.ShapeDtypeStruct((n, d), jnp.float32),
        compiler_params=_params("parallel"),
        name="attn_out",
    )(o, proj, x, w_out, ln_g, ln_b)


def kernel(x, a_w_in, a_b_in, a_vln_g, a_vln_b, a_w_s, a_b_s, a_w_out, kv_w, b_w_in, b_w_out, ln_g, ln_b):
    bsz, seq, d = x.shape
    n = bsz * seq
    bf16 = jnp.bfloat16
    n_a = a_w_in.shape[0]
    n_b = b_w_in.shape[0]

    assert n_a >= 1, "the attention layers read the bf16 stream written by the last gMLP layer"
    xf = x.reshape(n, d)
    xb = None
    for i in range(n_a):
        hact, w_out = _gmlp_in(xf, a_w_in[i], a_b_in[i][None, :], a_w_out[i])
        xf, xb = _gmlp_out(hact, xf, a_vln_g[i][None, :], a_vln_b[i][None, :], a_w_s[i],
                           a_b_s[i][:, :, None], w_out, ln_g[i][None, :], ln_b[i][None, :])

    idx = jnp.arange(ATT_T)
    neg_tri = jnp.where(idx[:, None] >= idx[None, :], -1.0, 0.0).astype(bf16)
    assert n_b == 1, "the fused k|v|q|gate projection supports exactly one attention layer"
    proj, w_out = _attn_in(xb, kv_w, b_w_in[0], b_w_out[0])
    o = _attention(proj, neg_tri, bsz, seq)
    xf = _attn_out(o, proj, xf, w_out, ln_g[n_a][None, :], ln_b[n_a][None, :])
    return xf.reshape(bsz, seq, d)
```

```python
import functools

import jax
import jax.numpy as jnp
from jax import lax
from jax.experimental import pallas as pl
from jax.experimental.pallas import tpu as pltpu

D_MODEL = 2048
DEPTH = 2
CHUNK = 128
A_WIDTH = 2 * D_MODEL
A_GROUPS = 8
A_GROUP_DIM = A_WIDTH // A_GROUPS
B_HEADS = 16
B_HEAD_DIM = D_MODEL // B_HEADS
B_WIDTH = B_HEADS * B_HEAD_DIM
DN_ALPHA = (2.0 * DEPTH) ** 0.25
LN_EPS = 1e-5

VMEM_LIMIT_BYTES = 56 * 1024 * 1024

PROJ_TM = 1024
PROJ_TN = 1024
PROJ_SUB = 256
PROJ_SLOTS = 3
WNEXT_BLOCKS = 32
MIX_TM = 256
LN_ROWS = 16
ATTN_OUT_TM = 512
ATT_T = 256
ATT_HEADS = 8
ATT_TILES = 4
ATT_LAG = 2

_GELU_C = 0.7978845608028654
_LOG2_E = 1.4426950408889634


def _gelu_tanh(x):
    inner = x * (_GELU_C + (_GELU_C * 0.044715) * (x * x))
    return (0.5 * x) * (1.0 + jnp.tanh(inner))


def _silu(x):
    half = 0.5 * x
    return half * (1.0 + jnp.tanh(half))


def _layer_norm(z, g, b):
    mu = jnp.mean(z, axis=-1, keepdims=True)
    zc = z - mu
    var = jnp.mean(zc * zc, axis=-1, keepdims=True)
    return zc * lax.rsqrt(var + LN_EPS) * g + b


def _params(*semantics):
    return pltpu.CompilerParams(dimension_semantics=semantics, vmem_limit_bytes=VMEM_LIMIT_BYTES)


def _project_columns(x_ref, w_ref, o_ref, acc_ref, epilogue, b_ref=None):
    groups = w_ref.shape[1] // PROJ_SUB

    def product(n):
        cols = slice(n * PROJ_SUB, (n + 1) * PROJ_SUB)
        acc_ref[n % PROJ_SLOTS] = jnp.dot(x_ref[...], w_ref[:, cols].astype(jnp.bfloat16),
                                          preferred_element_type=jnp.float32)

    def finish(m):
        cols = slice(m * PROJ_SUB, (m + 1) * PROJ_SUB)
        h = acc_ref[m % PROJ_SLOTS]
        if b_ref is not None:
            h = h + b_ref[:, cols]
        o_ref[:, cols] = epilogue(h).astype(o_ref.dtype)

    lag = PROJ_SLOTS - 1
    for n in range(groups + lag):
        if n < groups:
            product(n)
        if n >= lag:
            finish(n - lag)


def _proj_scratch():
    return pltpu.VMEM((PROJ_SLOTS, PROJ_TM, PROJ_SUB), jnp.float32)


def _round_rows(step, src_ref, dst_ref, blocks):
    @pl.when(step < blocks)
    def _():
        dst_ref[...] = src_ref[...].astype(jnp.bfloat16)


def _round_rows_specs(w, blocks, step_of):
    rows = w.shape[0] // blocks
    index = lambda *ids: (jnp.minimum(step_of(*ids), blocks - 1), 0)
    spec = pl.BlockSpec((rows, w.shape[1]), index)
    return spec, spec, jax.ShapeDtypeStruct(w.shape, jnp.bfloat16)


def _gmlp_in_kernel(x_ref, w_ref, b_ref, wnext_ref, o_ref, wnext_bf_ref, xb_ref, acc_ref, *,
                    silu_from, wnext_blocks):
    j = pl.program_id(1)
    _round_rows(pl.program_id(0) * pl.num_programs(1) + j, wnext_ref, wnext_bf_ref, wnext_blocks)

    @pl.when(j == 0)
    def _():
        xb_ref[...] = x_ref[...].astype(jnp.bfloat16)

    @pl.when(j < silu_from)
    def _():
        _project_columns(xb_ref, w_ref, o_ref, acc_ref, _gelu_tanh, b_ref)

    @pl.when(j >= silu_from)
    def _():
        _project_columns(xb_ref, w_ref, o_ref, acc_ref, _silu, b_ref)


def _gmlp_in(x, w, b, w_next):
    n, d = x.shape
    width = w.shape[1]
    grid = (n // PROJ_TM, width // PROJ_TN)
    wnext_in, wnext_out, wnext_shape = _round_rows_specs(
        w_next, WNEXT_BLOCKS, lambda i, j: i * grid[1] + j)
    return pl.pallas_call(
        functools.partial(_gmlp_in_kernel, silu_from=2 * A_WIDTH // PROJ_TN,
                          wnext_blocks=WNEXT_BLOCKS),
        grid=grid,
        in_specs=[
            pl.BlockSpec((PROJ_TM, d), lambda i, j: (i, 0)),
            pl.BlockSpec((d, PROJ_TN), lambda i, j: (0, j)),
            pl.BlockSpec((1, PROJ_TN), lambda i, j: (0, j)),
            wnext_in,
        ],
        out_specs=[pl.BlockSpec((PROJ_TM, PROJ_TN), lambda i, j: (i, j)), wnext_out],
        out_shape=[jax.ShapeDtypeStruct((n, width), jnp.bfloat16), wnext_shape],
        scratch_shapes=[pltpu.VMEM((PROJ_TM, d), jnp.bfloat16), _proj_scratch()],
        compiler_params=_params("arbitrary", "arbitrary"),
        name="gmlp_in",
    )(x, w, b, w_next)


def _gmlp_out_kernel(u_ref, v_ref, g_ref, x_ref, vg_ref, vb_ref, ws_ref, bs_ref, wo_ref,
                     lg_ref, lb_ref, o_ref, ob_ref, s_ref, vn_ref, y_ref):
    tm = u_ref.shape[0]
    for r in range(tm // LN_ROWS):
        band = slice(r * LN_ROWS, (r + 1) * LN_ROWS)
        vn_ref[band, :] = _layer_norm(v_ref[band, :].astype(jnp.float32), vg_ref[...],
                                      vb_ref[...]).astype(jnp.bfloat16)
    row = lax.broadcasted_iota(jnp.int32, (CHUNK, CHUNK), 0)
    col = lax.broadcasted_iota(jnp.int32, (CHUNK, CHUNK), 1)
    causal = col <= row
    for g in range(A_GROUPS):
        wt = jnp.where(causal, ws_ref[g], 0.0).astype(jnp.bfloat16)
        bias = bs_ref[g]
        cols = slice(g * A_GROUP_DIM, (g + 1) * A_GROUP_DIM)
        for c in range(tm // CHUNK):
            rows = slice(c * CHUNK, (c + 1) * CHUNK)
            mixed = jnp.dot(wt, vn_ref[rows, cols], preferred_element_type=jnp.float32) + bias
            gate = u_ref[rows, cols].astype(jnp.float32) * g_ref[rows, cols].astype(jnp.float32)
            s_ref[rows, cols] = (gate * mixed).astype(jnp.bfloat16)
    y_ref[...] = jnp.dot(s_ref[...], wo_ref[...], preferred_element_type=jnp.float32)
    for r in range(tm // LN_ROWS):
        band = slice(r * LN_ROWS, (r + 1) * LN_ROWS)
        out = _layer_norm(DN_ALPHA * x_ref[band, :] + y_ref[band, :], lg_ref[...], lb_ref[...])
        o_ref[band, :] = out
        ob_ref[band, :] = out.astype(jnp.bfloat16)


def _resident(shape):
    zeros = (0,) * len(shape)
    return pl.BlockSpec(shape, lambda i: zeros, pipeline_mode=pl.Buffered(1))


def _gmlp_out(hact, x, vln_g, vln_b, w_s, b_s, w_out, ln_g, ln_b):
    n, d = x.shape
    tm = MIX_TM
    return pl.pallas_call(
        _gmlp_out_kernel,
        grid=(n // tm,),
        in_specs=[
            pl.BlockSpec((tm, A_WIDTH), lambda i: (i, 0)),
            pl.BlockSpec((tm, A_WIDTH), lambda i: (i, 1)),
            pl.BlockSpec((tm, A_WIDTH), lambda i: (i, 2)),
            pl.BlockSpec((tm, d), lambda i: (i, 0)),
            _resident((1, A_WIDTH)),
            _resident((1, A_WIDTH)),
            _resident((A_GROUPS, CHUNK, CHUNK)),
            _resident((A_GROUPS, CHUNK, 1)),
            _resident((A_WIDTH, d)),
            _resident((1, d)),
            _resident((1, d)),
        ],
        out_specs=[
            pl.BlockSpec((tm, d), lambda i: (i, 0)),
            pl.BlockSpec((tm, d), lambda i: (i, 0)),
        ],
        out_shape=[
            jax.ShapeDtypeStruct((n, d), jnp.float32),
            jax.ShapeDtypeStruct((n, d), jnp.bfloat16),
        ],
        scratch_shapes=[pltpu.VMEM((tm, A_WIDTH), jnp.bfloat16), pltpu.VMEM((tm, A_WIDTH), jnp.bfloat16),
                        pltpu.VMEM((tm, d), jnp.float32)],
        compiler_params=_params("parallel"),
        name="gmlp_out",
    )(hact, hact, hact, x, vln_g, vln_b, w_s, b_s, w_out, ln_g, ln_b)


def _attn_in_kernel(x_ref, wkv_ref, wqg_ref, wnext_ref, o_ref, wnext_bf_ref, wb_ref, acc_ref, *,
                    q_from, gate_from, q_scale, wnext_blocks):
    j = pl.program_id(0)
    first_tile = pl.program_id(1) == 0
    _round_rows(j * pl.num_programs(1) + pl.program_id(1), wnext_ref, wnext_bf_ref, wnext_blocks)

    @pl.when(jnp.logical_and(first_tile, j < q_from))
    def _():
        wb_ref[...] = wkv_ref[...].astype(jnp.bfloat16)

    @pl.when(jnp.logical_and(first_tile, j >= q_from))
    def _():
        wb_ref[...] = wqg_ref[...].astype(jnp.bfloat16)

    @pl.when(j < q_from)
    def _():
        _project_columns(x_ref, wb_ref, o_ref, acc_ref, lambda h: h)

    @pl.when(jnp.logical_and(j >= q_from, j < gate_from))
    def _():
        _project_columns(x_ref, wb_ref, o_ref, acc_ref, lambda h: h * q_scale)

    @pl.when(j >= gate_from)
    def _():
        _project_columns(x_ref, wb_ref, o_ref, acc_ref, _silu)


def _attn_in(xb, w_kv, w_qg, w_next):
    n, d = xb.shape
    kv_tiles = w_kv.shape[1] // PROJ_TN
    qg_tiles = w_qg.shape[1] // PROJ_TN
    grid = (kv_tiles + qg_tiles, n // PROJ_TM)
    wnext_in, wnext_out, wnext_shape = _round_rows_specs(
        w_next, WNEXT_BLOCKS, lambda j, i: j * grid[1] + i)
    kernel = functools.partial(
        _attn_in_kernel,
        q_from=kv_tiles,
        gate_from=kv_tiles + qg_tiles // 2,
        q_scale=B_HEAD_DIM ** -0.5 * _LOG2_E,
        wnext_blocks=WNEXT_BLOCKS,
    )
    return pl.pallas_call(
        kernel,
        grid=grid,
        in_specs=[
            pl.BlockSpec((PROJ_TM, d), lambda j, i: (i, 0)),
            pl.BlockSpec((d, PROJ_TN), lambda j, i: (0, jnp.minimum(j, kv_tiles - 1))),
            pl.BlockSpec((d, PROJ_TN), lambda j, i: (0, jnp.maximum(j - kv_tiles, 0))),
            wnext_in,
        ],
        out_specs=[pl.BlockSpec((PROJ_TM, PROJ_TN), lambda j, i: (i, j)), wnext_out],
        out_shape=[jax.ShapeDtypeStruct((n, w_kv.shape[1] + w_qg.shape[1]), jnp.bfloat16), wnext_shape],
        scratch_shapes=[pltpu.VMEM((d, PROJ_TN), jnp.bfloat16), _proj_scratch()],
        compiler_params=_params("arbitrary", "arbitrary"),
        name="attn_in",
    )(xb, w_kv, w_qg, w_next)


_EXP2_CLAMP = 64.0


def _softplus2(z):
    return jnp.maximum(z, jnp.log2(1.0 + jnp.exp2(jnp.minimum(z, _EXP2_CLAMP))))


def _attn_kernel(q_ref, k_ref, v_ref, m_ref, o_ref, c_ref, acc_ref):
    t = ATT_T
    dh = B_HEAD_DIM
    qi = pl.program_id(2)
    neg_tri = m_ref[...]

    def sweep(kbs, diagonal, fresh=False):
        heads = range(ATT_HEADS)
        starts = [pl.multiple_of(kb * t, t) for kb in kbs]
        chains = [(i, h) for i in range(len(kbs)) for h in heads]
        if any(diagonal):
            past = (lax.broadcasted_iota(jnp.int32, (t, t), 1)
                    < lax.broadcasted_iota(jnp.int32, (t, t), 0))
        z, suffix = {}, {}
        if fresh:
            cs = [jnp.zeros(c_ref.shape[1:], jnp.float32) for h in heads]
            accs = [jnp.zeros(acc_ref.shape[1:], jnp.float32) for h in heads]
        else:
            cs = [c_ref[h] for h in heads]
            accs = [acc_ref[h] for h in heads]

        def scores(i, h):
            z[i, h] = lax.dot_general(q_ref[:, h * dh:(h + 1) * dh],
                                      k_ref[pl.ds(starts[i], t), h * dh:(h + 1) * dh],
                                      (((1,), (1,)), ((), ())), preferred_element_type=jnp.float32)

        def suffix_sum(i, h):
            sp = _softplus2(z[i, h])
            if diagonal[i]:
                sp = jnp.where(past, sp, 0.0)
            suffix[i, h] = jnp.dot(sp.astype(jnp.bfloat16), neg_tri,
                                   preferred_element_type=jnp.float32)

        def values(i, h):
            w = jnp.exp2(z[i, h] + suffix[i, h] + cs[h])
            if diagonal[i]:
                w = jnp.where(past, w, 0.0)
            accs[h] = accs[h] + jnp.dot(w.astype(jnp.bfloat16),
                                        v_ref[pl.ds(starts[i], t), h * dh:(h + 1) * dh],
                                        preferred_element_type=jnp.float32)
            cs[h] = cs[h] + suffix[i, h][:, 0:1]

        n = len(chains)
        for step in range(n + 2 * ATT_LAG):
            if 0 <= step - ATT_LAG < n:
                suffix_sum(*chains[step - ATT_LAG])
            if step < n:
                scores(*chains[step])
            if 0 <= step - 2 * ATT_LAG < n:
                values(*chains[step - 2 * ATT_LAG])
        for h in heads:
            c_ref[h] = cs[h]
            acc_ref[h] = accs[h]

    first = qi % ATT_TILES + 1
    for r in range(1, ATT_TILES + 1):
        @pl.when(first == r)
        def _(r=r):
            sweep([qi - i for i in range(r)], [i == 0 for i in range(r)], fresh=True)

    top = qi + 1 - first

    @pl.loop(0, top // ATT_TILES)
    def _(p):
        sweep([top - 1 - ATT_TILES * p - i for i in range(ATT_TILES)], [False] * ATT_TILES)

    for h in range(ATT_HEADS):
        o_ref[:, h * dh:(h + 1) * dh] = acc_ref[h].astype(o_ref.dtype)


def _attention(proj, neg_tri, bsz, seq):
    t = ATT_T
    nq = seq // t
    hw = ATT_HEADS * B_HEAD_DIM
    nh = B_WIDTH // hw
    return pl.pallas_call(
        _attn_kernel,
        grid=(bsz, nh, nq),
        in_specs=[
            pl.BlockSpec((t, hw), lambda b, h, i: (b * nq + i, 2 * nh + h)),
            pl.BlockSpec((seq, hw), lambda b, h, i: (b, h)),
            pl.BlockSpec((seq, hw), lambda b, h, i: (b, nh + h)),
            pl.BlockSpec((t, t), lambda b, h, i: (0, 0)),
        ],
        out_specs=pl.BlockSpec((t, hw), lambda b, h, i: (b * nq + i, h)),
        out_shape=jax.ShapeDtypeStruct((bsz * seq, B_WIDTH), jnp.bfloat16),
        scratch_shapes=[pltpu.VMEM((ATT_HEADS, t, 1), jnp.float32),
                        pltpu.VMEM((ATT_HEADS, t, B_HEAD_DIM), jnp.float32)],
        compiler_params=_params("parallel", "parallel", "arbitrary"),
        name="stick_breaking_attention",
    )(proj, proj, proj, neg_tri)


def _attn_out_kernel(o_ref, g_ref, x_ref, wo_ref, lg_ref, lb_ref, out_ref):
    s = (o_ref[...].astype(jnp.float32) * g_ref[...].astype(jnp.float32)).astype(jnp.bfloat16)
    y = jnp.dot(s, wo_ref[...], preferred_element_type=jnp.float32)
    out_ref[...] = _layer_norm(DN_ALPHA * x_ref[...] + y, lg_ref[...], lb_ref[...])


def _attn_out(o, proj, x, w_out, ln_g, ln_b):
    n, d = x.shape
    tm = ATTN_OUT_TM
    return pl.pallas_call(
        _attn_out_kernel,
        grid=(n // tm,),
        in_specs=[
            pl.BlockSpec((tm, B_WIDTH), lambda i: (i, 0)),
            pl.BlockSpec((tm, B_WIDTH), lambda i: (i, 3)),
            pl.BlockSpec((tm, d), lambda i: (i, 0)),
            _resident((B_WIDTH, d)),
            _resident((1, d)),
            _resident((1, d)),
        ],
        out_specs=pl.BlockSpec((tm, d), lambda i: (i, 0)),
        out_shape=jax.ShapeDtypeStruct((n, d), jnp.float32),
        compiler_params=_params("parallel"),
        name="attn_out",
    )(o, proj, x, w_out, ln_g, ln_b)


def kernel(x, a_w_in, a_b_in, a_vln_g, a_vln_b, a_w_s, a_b_s, a_w_out, kv_w, b_w_in, b_w_out, ln_g, ln_b):
    bsz, seq, d = x.shape
    n = bsz * seq
    bf16 = jnp.bfloat16
    n_a = a_w_in.shape[0]
    n_b = b_w_in.shape[0]

    assert n_a >= 1, "the attention layers read the bf16 stream written by the last gMLP layer"
    xf = x.reshape(n, d)
    xb = None
    for i in range(n_a):
        hact, w_out = _gmlp_in(xf, a_w_in[i], a_b_in[i][None, :], a_w_out[i])
        xf, xb = _gmlp_out(hact, xf, a_vln_g[i][None, :], a_vln_b[i][None, :], a_w_s[i],
                           a_b_s[i][:, :, None], w_out, ln_g[i][None, :], ln_b[i][None, :])

    idx = jnp.arange(ATT_T)
    neg_tri = jnp.where(idx[:, None] >= idx[None, :], -1.0, 0.0).astype(bf16)
    assert n_b == 1, "the fused k|v|q|gate projection supports exactly one attention layer"
    proj, w_out = _attn_in(xb, kv_w, b_w_in[0], b_w_out[0])
    o = _attention(proj, neg_tri, bsz, seq)
    xf = _attn_out(o, proj, xf, w_out, ln_g[n_a][None, :], ln_b[n_a][None, :])
    return xf.reshape(bsz, seq, d)
```
